```python
import jax, jax.numpy as jnp
from jax import lax
import numpy as np

D_MODEL = 1024
BATCH = 4
SEQ = 4096
DEPTH = 1

N_MEM = 256
MIX_WIDTH = D_MODEL
FOX_HEADS = 8
FOX_HEAD_DIM = 64
FOX_WIDTH = FOX_HEADS * FOX_HEAD_DIM
GMLP_GROUPS = 8
GMLP_GROUP_DIM = 64
GMLP_WIDTH = GMLP_GROUPS * GMLP_GROUP_DIM
CHUNK = 128
Q_BLOCK = 128
Q_OFF = 0
K_OFF = Q_OFF + FOX_WIDTH
V_OFF = K_OFF + FOX_WIDTH
F_OFF = V_OFF + FOX_WIDTH
UV_OFF = F_OFF + FOX_HEADS
IN_COLS = UV_OFF + 2 * GMLP_WIDTH
CA_HEADS = 4
CA_HEAD_DIM = D_MODEL // CA_HEADS
D_FF = 2816
EPS = 1e-6

kernel_name = "hybrid_fox_gmlp_macaron_memxattn"


def rms_norm(x, g):
    xf = x.astype(jnp.float32)
    y = xf * lax.rsqrt(jnp.mean(xf * xf, axis=-1, keepdims=True) + EPS)
    return (y * g.astype(jnp.float32)).astype(x.dtype)


def swiglu(h, w_in, w_out):
    gu = h @ w_in
    g, u = jnp.split(gu, 2, axis=-1)
    return (jax.nn.silu(g) * u) @ w_out


def fox_attention(q, k, v, log_f):
    S = q.shape[2]
    scale = FOX_HEAD_DIM ** -0.5
    c = jnp.cumsum(log_f, axis=-1)
    neg = jnp.finfo(jnp.float32).min
    outs = []
    for i in range(S // Q_BLOCK):
        q0, q1 = i * Q_BLOCK, (i + 1) * Q_BLOCK
        qb, kb, vb = q[:, :, q0:q1], k[:, :, :q1], v[:, :, :q1]
        s = (jnp.einsum('bhqd,bhkd->bhqk', qb, kb).astype(jnp.float32) * scale
             + c[:, :, q0:q1, None] - c[:, :, None, :q1])
        mask = (q0 + jnp.arange(Q_BLOCK))[:, None] >= jnp.arange(q1)[None, :]
        s = jnp.where(mask, s, neg)
        p = jax.nn.softmax(s, axis=-1)
        outs.append(jnp.einsum('bhqk,bhkd->bhqd', p.astype(vb.dtype), vb))
    return jnp.concatenate(outs, axis=2)


def spatial_gating(u, v, w_s, b_s):
    B, S, G, Dg = v.shape
    vc = v.reshape(B, S // CHUNK, CHUNK, G, Dg)
    tril = jnp.tril(jnp.ones((CHUNK, CHUNK), dtype=bool))
    w = jnp.where(tril[None], w_s, jnp.zeros_like(w_s))
    mixed = jnp.einsum('gts,bcsgd->bctgd', w, vc) + b_s.T[None, None, :, :, None]
    return u * mixed.reshape(B, S, G, Dg)


def mem_cross_attention(h, m, w_cq, w_ckv, g_cq, g_ck, w_co):
    B, S, _ = h.shape
    M = m.shape[1]
    q = (h @ w_cq).reshape(B, S, CA_HEADS, CA_HEAD_DIM)
    kv = m @ w_ckv
    k, v = jnp.split(kv, 2, axis=-1)
    k = k.reshape(B, M, CA_HEADS, CA_HEAD_DIM)
    v = v.reshape(B, M, CA_HEADS, CA_HEAD_DIM)
    q = rms_norm(q, g_cq)
    k = rms_norm(k, g_ck)
    s = jnp.einsum('bqhd,bkhd->bhqk', q, k).astype(jnp.float32) * (CA_HEAD_DIM ** -0.5)
    p = jax.nn.softmax(s, axis=-1)
    o = jnp.einsum('bhqk,bkhd->bqhd', p.astype(v.dtype), v).reshape(B, S, D_MODEL)
    return o @ w_co


def setup_inputs(seed: int = 0) -> dict:
    key = jax.random.key(seed)
    ks = jax.random.split(key, 32)
    L = DEPTH

    def w(k, shape, fan_in):
        return jax.random.normal(k, shape, jnp.float32) * (fan_in ** -0.5)

    def g(k, shape):
        return 1.0 + 0.02 * jax.random.normal(k, shape, jnp.float32)

    b_f = (jnp.linspace(1.0, 6.0, FOX_HEADS, dtype=jnp.float32)[None, :]
           + 0.1 * jax.random.normal(ks[7], (L, FOX_HEADS), jnp.float32))
    return {
        "x": jax.random.normal(ks[0], (BATCH, SEQ, D_MODEL), jnp.float32),
        "mem": jax.random.normal(ks[1], (BATCH, N_MEM, D_MODEL), jnp.float32),
        "g_ffn1": g(ks[2], (L, D_MODEL)),
        "w_ffn1_in": w(ks[3], (L, D_MODEL, 2 * D_FF), D_MODEL),
        "w_ffn1_out": w(ks[4], (L, D_FF, D_MODEL), D_FF),
        "g_mix": g(ks[5], (L, D_MODEL)),
        "w_in": w(ks[6], (L, D_MODEL, IN_COLS), D_MODEL),
        "b_f": b_f,
        "g_q": g(ks[8], (L, FOX_HEAD_DIM)),
        "g_k": g(ks[9], (L, FOX_HEAD_DIM)),
        "g_sgu": g(ks[10], (L, GMLP_WIDTH)),
        "w_s": w(ks[11], (L, GMLP_GROUPS, CHUNK, CHUNK), CHUNK),
        "b_s": g(ks[12], (L, GMLP_GROUPS, CHUNK)),
        "g_fox_o": g(ks[13], (L, FOX_WIDTH)),
        "g_gmlp_o": g(ks[14], (L, GMLP_WIDTH)),
        "w_out": w(ks[15], (L, MIX_WIDTH, D_MODEL), MIX_WIDTH),
        "g_ca": g(ks[16], (L, D_MODEL)),
        "g_mem": g(ks[17], (L, D_MODEL)),
        "w_cq": w(ks[18], (L, D_MODEL, D_MODEL), D_MODEL),
        "w_ckv": w(ks[19], (L, D_MODEL, 2 * D_MODEL), D_MODEL),
        "g_cq": g(ks[20], (L, CA_HEAD_DIM)),
        "g_ck": g(ks[21], (L, CA_HEAD_DIM)),
        "w_co": w(ks[22], (L, D_MODEL, D_MODEL), D_MODEL),
        "g_ffn2": g(ks[23], (L, D_MODEL)),
        "w_ffn2_in": w(ks[24], (L, D_MODEL, 2 * D_FF), D_MODEL),
        "w_ffn2_out": w(ks[25], (L, D_FF, D_MODEL), D_FF),
    }


def reference(x, mem, g_ffn1, w_ffn1_in, w_ffn1_out, g_mix, w_in, b_f, g_q, g_k,
              g_sgu, w_s, b_s, g_fox_o, g_gmlp_o, w_out, g_ca, g_mem, w_cq, w_ckv,
              g_cq, g_ck, w_co, g_ffn2, w_ffn2_in, w_ffn2_out):
    B, S, _ = x.shape
    for l in range(DEPTH):
        x = x + 0.5 * swiglu(rms_norm(x, g_ffn1[l]), w_ffn1_in[l], w_ffn1_out[l])

        h = rms_norm(x, g_mix[l])
        z = h @ w_in[l]
        q = z[..., Q_OFF:K_OFF].reshape(B, S, FOX_HEADS, FOX_HEAD_DIM)
        k = z[..., K_OFF:V_OFF].reshape(B, S, FOX_HEADS, FOX_HEAD_DIM)
        v = z[..., V_OFF:F_OFF].reshape(B, S, FOX_HEADS, FOX_HEAD_DIM)
        f_logit = z[..., F_OFF:UV_OFF]
        uv = z[..., UV_OFF:]

        q = rms_norm(q, g_q[l])
        k = rms_norm(k, g_k[l])
        log_f = jax.nn.log_sigmoid(f_logit.astype(jnp.float32) + b_f[l].astype(jnp.float32))
        attn = fox_attention(q.transpose(0, 2, 1, 3), k.transpose(0, 2, 1, 3),
                             v.transpose(0, 2, 1, 3), log_f.transpose(0, 2, 1))
        attn = attn.transpose(0, 2, 1, 3).reshape(B, S, FOX_WIDTH)

        uv = jax.nn.gelu(uv)
        u, vg = jnp.split(uv, 2, axis=-1)
        vg = rms_norm(vg, g_sgu[l])
        sgu = spatial_gating(u.reshape(B, S, GMLP_GROUPS, GMLP_GROUP_DIM),
                             vg.reshape(B, S, GMLP_GROUPS, GMLP_GROUP_DIM),
                             w_s[l], b_s[l]).reshape(B, S, GMLP_WIDTH)

        y = jnp.concatenate([rms_norm(attn, g_fox_o[l]), rms_norm(sgu, g_gmlp_o[l])], axis=-1)
        x = x + y @ w_out[l]

        x = x + mem_cross_attention(rms_norm(x, g_ca[l]), rms_norm(mem, g_mem[l]),
                                    w_cq[l], w_ckv[l], g_cq[l], g_ck[l], w_co[l])

        x = x + 0.5 * swiglu(rms_norm(x, g_ffn2[l]), w_ffn2_in[l], w_ffn2_out[l])
    return x
```

```python
import functools

import jax
import jax.numpy as jnp
import numpy as np
from jax import lax
from jax.experimental import pallas as pl
from jax.experimental.pallas import tpu as pltpu

F32 = jnp.float32
BF16 = jnp.bfloat16
EPS = 1e-6
LOG2E = 1.4426950408889634
MASKED = -1e30

FOX_HEADS = 8
FOX_DH = 64
FOX_W = FOX_HEADS * FOX_DH
GMLP_GROUPS = 8
GMLP_DG = 64
GMLP_W = GMLP_GROUPS * GMLP_DG
CHUNK = 128
CA_HEADS = 4

LANES = 128
VMEM_LIMIT_BYTES = 56 * 1024 * 1024

TM = 512
TQ = 256
TK = 256
FC = 256
PAIRS = FOX_HEADS // 2
BIAS_ROWS = 32


def _rms(x):
    return x * lax.rsqrt(jnp.mean(x * x, axis=-1, keepdims=True) + EPS)


def _dot(a, b):
    return jnp.dot(a, b, preferred_element_type=F32)


def _dot_nt(a, b):
    return lax.dot_general(a, b, (((1,), (1,)), ((), ())), preferred_element_type=F32)


def _split3(x):
    hi = x.astype(BF16).astype(F32)
    mid = (x - hi).astype(BF16).astype(F32)
    lo = (x - hi - mid).astype(BF16).astype(F32)
    return hi, mid, lo


def _resident(shape):
    return pl.BlockSpec(shape, lambda *_: (0,) * len(shape), pipeline_mode=pl.Buffered(1))


def _params(n_axes):
    return pltpu.CompilerParams(dimension_semantics=("arbitrary",) * n_axes,
                                vmem_limit_bytes=VMEM_LIMIT_BYTES)


def _swiglu_half_step(x, g_ref, wgu_ref, wo_ref, act_ref):
    d_ff = wo_ref.shape[0]
    h = (_rms(x) * g_ref[...]).astype(BF16)
    for c in range(d_ff // FC):
        gate = _dot(h, wgu_ref[:, c * FC:(c + 1) * FC])
        up = _dot(h, wgu_ref[:, d_ff + c * FC:d_ff + (c + 1) * FC])
        act_ref[:, c * FC:(c + 1) * FC] = (gate * jax.nn.sigmoid(gate) * up).astype(BF16)
    return x + 0.5 * _dot(act_ref[...], wo_ref[...])


def _ffn_kernel(x_ref, g_ref, wgu_ref, wo_ref, o_ref, act_ref):
    o_ref[...] = _swiglu_half_step(x_ref[...], g_ref, wgu_ref, wo_ref, act_ref)


def _ffn(x, g, wgu, wo):
    t, d = x.shape
    d_ff = wo.shape[0]
    row = pl.BlockSpec((TM, d), lambda r: (r, 0))
    return pl.pallas_call(
        _ffn_kernel,
        grid=(t // TM,),
        in_specs=[row, _resident((1, d)), _resident((d, 2 * d_ff)), _resident((d_ff, d))],
        out_specs=row,
        out_shape=jax.ShapeDtypeStruct((t, d), F32),
        scratch_shapes=[pltpu.VMEM((TM, d_ff), BF16)],
        compiler_params=_params(1),
        name="ffn",
    )(x, g, wgu, wo)


def _gelu_tanh(x):
    return x * (0.5 * (1.0 + jnp.tanh(np.float32(np.sqrt(2.0 / np.pi)) * (x + 0.044715 * (x * x * x)))))


def _proj_kernel(x_ref, g_mix_ref, w_nn_ref, w_nt_ref, gq_ref, gk_ref, bf_ref, headsum_ref, triu_ref,
                 place_ref, g_sgu_ref, ws_ref, bs_ref, g_go_ref,
                 qT_ref, cT_ref, k_ref, cb_ref, vT_ref, yg_ref,
                 carry_ref, sgu_ref, *, blocks_per_seq):
    h = (_rms(x_ref[...]) * g_mix_ref[...]).astype(BF16)

    zT = _dot_nt(w_nt_ref[...], h)
    for hd in range(FOX_HEADS):
        blk = zT[hd * FOX_DH:(hd + 1) * FOX_DH, :]
        inv = lax.rsqrt(jnp.mean(blk * blk, axis=0, keepdims=True) + EPS)
        qT_ref[0, hd * FOX_DH:(hd + 1) * FOX_DH, :] = (blk * inv * gq_ref[...]).astype(BF16)
    for c in range(TM // TK):
        vT_ref[0, c] = zT[FOX_W:2 * FOX_W, c * TK:(c + 1) * TK].astype(BF16)

    @pl.when(pl.program_id(0) % blocks_per_seq == 0)
    def _():
        carry_ref[...] = jnp.zeros_like(carry_ref)

    f = zT[2 * FOX_W:2 * FOX_W + FOX_HEADS, :] + bf_ref[...]
    logf = (jnp.minimum(f, 0.0) - jnp.log1p(jnp.exp(-jnp.abs(f)))) * LOG2E
    zeros8 = jnp.zeros((8, TM), F32)
    parts = jnp.concatenate(_split3(logf) + (zeros8,), axis=0).astype(BF16)
    cs = _dot(parts, triu_ref[...])
    c = cs[0:8] + cs[8:16] + cs[16:24] + carry_ref[:, 0:1]
    carry_ref[...] = jnp.broadcast_to(c[:, TM - 1:TM], carry_ref.shape)
    cT_ref[0] = c

    one_row = jnp.where(lax.broadcasted_iota(jnp.int32, (8, TM), 0) == 0, 1.0, 0.0)
    cparts = jnp.concatenate(_split3(c) + (one_row,), axis=0).astype(BF16)
    cb_ref[...] = _dot(place_ref[...], cparts).T.astype(BF16)

    z = _dot(h, w_nn_ref[...])
    k = z[:, :FOX_W]
    ksum = _dot((k * k).astype(BF16), headsum_ref[...])
    k_ref[...] = (k * lax.rsqrt(ksum * (1.0 / FOX_DH) + EPS) * gk_ref[...]).astype(BF16)

    u = _gelu_tanh(z[:, FOX_W:FOX_W + GMLP_W])
    vg = _gelu_tanh(z[:, FOX_W + GMLP_W:])
    vgn = (_rms(vg) * g_sgu_ref[...]).astype(BF16)
    lane = lax.broadcasted_iota(jnp.int32, (CHUNK, LANES), 1)
    tril = (lax.broadcasted_iota(jnp.int32, (CHUNK, CHUNK), 0)
            >= lax.broadcasted_iota(jnp.int32, (CHUNK, CHUNK), 1))
    n_chunks = TM // CHUNK
    for p in range(GMLP_GROUPS // 2):
        lo_blocks, hi_blocks = [], []
        for ch in range(n_chunks):
            vp = vgn[ch * CHUNK:(ch + 1) * CHUNK, p * LANES:(p + 1) * LANES]
            lo_blocks.append(jnp.where(lane < GMLP_DG, vp, jnp.zeros_like(vp)))
            hi_blocks.append(jnp.where(lane >= GMLP_DG, vp, jnp.zeros_like(vp)))
        vcat = jnp.concatenate([jnp.concatenate(lo_blocks, axis=1),
                                jnp.concatenate(hi_blocks, axis=1)], axis=0)
        w0 = ws_ref[2 * p]
        w1 = ws_ref[2 * p + 1]
        wpair = jnp.concatenate([jnp.where(tril, w0, jnp.zeros_like(w0)),
                                 jnp.where(tril, w1, jnp.zeros_like(w1))], axis=1)
        mixed = _dot(wpair, vcat)
        bias = bs_ref[:, p * LANES:(p + 1) * LANES]
        for ch in range(n_chunks):
            sgu_ref[ch * CHUNK:(ch + 1) * CHUNK, p * LANES:(p + 1) * LANES] = (
                u[ch * CHUNK:(ch + 1) * CHUNK, p * LANES:(p + 1) * LANES]
                * (mixed[:, ch * LANES:(ch + 1) * LANES] + bias))
    yg_ref[...] = (_rms(sgu_ref[...]) * g_go_ref[...]).astype(BF16)


def _proj(x, batch, seq, g_mix, w_nn, w_nt, gq, gk, bfc, headsum, triu, place, g_sgu, ws, bs, g_go):
    t, d = x.shape
    bps = seq // TM
    n_nt = w_nt.shape[0]

    def rows(width):
        return pl.BlockSpec((TM, width), lambda r: (r, 0))

    kern = functools.partial(_proj_kernel, blocks_per_seq=bps)
    return pl.pallas_call(
        kern,
        grid=(t // TM,),
        in_specs=[rows(d), _resident((1, d)), _resident((d, w_nn.shape[1])), _resident((n_nt, d)),
                  _resident((FOX_DH, TM)), _resident((1, FOX_W)), _resident((FOX_HEADS, TM)),
                  _resident((FOX_W, FOX_W)), _resident((TM, TM)), _resident((PAIRS * BIAS_ROWS, 32)),
                  _resident((1, GMLP_W)), _resident((GMLP_GROUPS, CHUNK, CHUNK)),
                  _resident((CHUNK, GMLP_W)), _resident((1, GMLP_W))],
        out_specs=[pl.BlockSpec((1, FOX_W, TM), lambda r: (r // bps, 0, r % bps)),
                   pl.BlockSpec((1, FOX_HEADS, TM), lambda r: (r // bps, 0, r % bps)),
                   rows(FOX_W), rows(LANES),
                   pl.BlockSpec((1, TM // TK, FOX_W, TK), lambda r: (r // bps, r % bps, 0, 0)),
                   rows(GMLP_W)],
        out_shape=[jax.ShapeDtypeStruct((batch, FOX_W, seq), BF16),
                   jax.ShapeDtypeStruct((batch, FOX_HEADS, seq), F32),
                   jax.ShapeDtypeStruct((t, FOX_W), BF16),
                   jax.ShapeDtypeStruct((t, LANES), BF16),
                   jax.ShapeDtypeStruct((batch, seq // TK, FOX_W, TK), BF16),
                   jax.ShapeDtypeStruct((t, GMLP_W), BF16)],
        scratch_shapes=[pltpu.VMEM((FOX_HEADS, LANES), F32), pltpu.VMEM((TM, GMLP_W), F32)],
        compiler_params=_params(1),
        name="proj",
    )(x, g_mix, w_nn, w_nt, gq, gk, bfc, headsum, triu, place, g_sgu, ws, bs, g_go)


def _fox_kernel(qT_ref, cT_ref, k_ref, cb_ref, vT_ref, o_ref, qq_ref, acc_ref):
    i = pl.program_id(1)

    row = lax.broadcasted_iota(jnp.int32, (LANES, TQ), 0)
    for p in range(PAIRS):
        qpair = qT_ref[0, p * LANES:(p + 1) * LANES, :]
        for e in range(2):
            hd = 2 * p + e
            own = (row >= FOX_DH * e) & (row < FOX_DH * (e + 1))
            qpart = jnp.where(own, qpair, jnp.zeros_like(qpair))
            hi, mid, lo = _split3(cT_ref[0, hd:hd + 1, :])
            base = BIAS_ROWS * p
            sel = (row >= base + 3 * e) & (row < base + 3 * e + 3)
            bias = jnp.where(sel, 1.0, 0.0)
            bias = jnp.where(row == base + 6, hi, bias)
            bias = jnp.where(row == base + 7, mid, bias)
            bias = jnp.where(row == base + 8, lo, bias)
            qq_ref[p, :, e * TQ:(e + 1) * TQ] = jnp.concatenate([qpart, bias.astype(BF16)], axis=0)

    acc_ref[...] = jnp.zeros_like(acc_ref)
    future = (lax.broadcasted_iota(jnp.int32, (TK, 2 * TQ), 0)
              > lax.broadcasted_iota(jnp.int32, (TK, 2 * TQ), 1) % TQ)

    def step(j, carry, diagonal):
        ms, ls = carry
        k0 = pl.multiple_of(j * TK, TK)
        cbj = cb_ref[0, pl.ds(k0, TK), :]
        new_m, new_l = [], []
        for p in range(PAIRS):
            kk = jnp.concatenate([k_ref[0, pl.ds(k0, TK), p * LANES:(p + 1) * LANES], cbj], axis=1)
            s = _dot(kk, qq_ref[p])
            if diagonal:
                s = jnp.where(future, MASKED, s)
            m = jnp.maximum(ms[p], jnp.max(s, axis=0, keepdims=True))
            alpha = jnp.exp2(ms[p] - m)
            pexp = jnp.exp2(s - m)
            new_l.append(alpha * ls[p] + jnp.sum(pexp, axis=0, keepdims=True))
            new_m.append(m)
            pb = pexp.astype(BF16)
            for e in range(2):
                hd = 2 * p + e
                pv = _dot(vT_ref[0, j, hd * FOX_DH:(hd + 1) * FOX_DH, :], pb[:, e * TQ:(e + 1) * TQ])
                acc_ref[hd] = alpha[:, e * TQ:(e + 1) * TQ] * acc_ref[hd] + pv
        return tuple(new_m), tuple(new_l)

    init = (tuple(jnp.full((1, 2 * TQ), MASKED, F32) for _ in range(PAIRS)),
            tuple(jnp.zeros((1, 2 * TQ), F32) for _ in range(PAIRS)))
    carry = lax.fori_loop(0, i, lambda j, cr: step(j, cr, False), init)
    _, ls = step(i, carry, True)

    outs = []
    for hd in range(FOX_HEADS):
        p, e = divmod(hd, 2)
        outs.append(acc_ref[hd] * (1.0 / ls[p][:, e * TQ:(e + 1) * TQ]))
    o_ref[0] = jnp.concatenate(outs, axis=0).T


def _fox(qT, cT, k, cb, vT):
    batch, _, seq = qT.shape
    return pl.pallas_call(
        _fox_kernel,
        grid=(batch, seq // TQ),
        in_specs=[pl.BlockSpec((1, FOX_W, TQ), lambda b, i: (b, 0, i)),
                  pl.BlockSpec((1, FOX_HEADS, TQ), lambda b, i: (b, 0, i)),
                  pl.BlockSpec((1, seq, FOX_W), lambda b, i: (b, 0, 0)),
                  pl.BlockSpec((1, seq, LANES), lambda b, i: (b, 0, 0)),
                  pl.BlockSpec((1, seq // TK, FOX_W, TK), lambda b, i: (b, 0, 0, 0))],
        out_specs=pl.BlockSpec((1, TQ, FOX_W), lambda b, i: (b, i, 0)),
        out_shape=jax.ShapeDtypeStruct((batch, seq, FOX_W), F32),
        scratch_shapes=[pltpu.VMEM((PAIRS, 2 * LANES, 2 * TQ), BF16),
                        pltpu.VMEM((FOX_HEADS, FOX_DH, TQ), F32)],
        compiler_params=_params(2),
        name="fox",
    )(qT, cT, k, cb, vT)


def _memkv_kernel(mem_ref, g_mem_ref, w_ckT_ref, w_cv_ref, g_ck_ref, kcT_ref, vc_ref):
    mn = (_rms(mem_ref[0]) * g_mem_ref[...]).astype(BF16)
    kT = _dot_nt(w_ckT_ref[...], mn)
    dh = kT.shape[0] // CA_HEADS
    for hd in range(CA_HEADS):
        blk = kT[hd * dh:(hd + 1) * dh, :]
        inv = lax.rsqrt(jnp.mean(blk * blk, axis=0, keepdims=True) + EPS)
        kcT_ref[0, hd * dh:(hd + 1) * dh, :] = (blk * inv * g_ck_ref[...]).astype(BF16)
    vc_ref[0] = _dot(mn, w_cv_ref[...]).astype(BF16)


def _memkv(mem, g_mem, w_ckT, w_cv, g_ck_col):
    batch, m, d = mem.shape
    return pl.pallas_call(
        _memkv_kernel,
        grid=(batch,),
        in_specs=[pl.BlockSpec((1, m, d), lambda b: (b, 0, 0)), _resident((1, d)), _resident((d, d)),
                  _resident((d, d)), _resident((d // CA_HEADS, m))],
        out_specs=[pl.BlockSpec((1, d, m), lambda b: (b, 0, 0)), pl.BlockSpec((1, m, d), lambda b: (b, 0, 0))],
        out_shape=[jax.ShapeDtypeStruct((batch, d, m), BF16), jax.ShapeDtypeStruct((batch, m, d), BF16)],
        compiler_params=_params(1),
        name="memkv",
    )(mem, g_mem, w_ckT, w_cv, g_ck_col)


def _tail_kernel(x_ref, attn_ref, yg_ref, g_fo_ref, w_out_ref, g_ca_ref, w_cq_ref, g_cq_ref, kcT_ref, vc_ref,
                 w_co_ref, g_f2_ref, wgu_ref, wo_ref, o_ref, act_ref):
    a = (_rms(attn_ref[...]) * g_fo_ref[...]).astype(BF16)
    x2 = x_ref[...] + _dot(jnp.concatenate([a, yg_ref[...]], axis=1), w_out_ref[...])

    hq = (_rms(x2) * g_ca_ref[...]).astype(BF16)
    qc = _dot(hq, w_cq_ref[...])
    dh = qc.shape[1] // CA_HEADS
    outs = []
    for hd in range(CA_HEADS):
        qh = (_rms(qc[:, hd * dh:(hd + 1) * dh]) * g_cq_ref[...]).astype(BF16)
        s = _dot(qh, kcT_ref[0, hd * dh:(hd + 1) * dh, :])
        pexp = jnp.exp2(s - jnp.max(s, axis=-1, keepdims=True))
        denom = jnp.sum(pexp, axis=-1, keepdims=True)
        o = _dot(pexp.astype(BF16), vc_ref[0, :, hd * dh:(hd + 1) * dh]) * (1.0 / denom)
        outs.append(o.astype(BF16))
    x3 = x2 + _dot(jnp.concatenate(outs, axis=1), w_co_ref[...])

    o_ref[...] = _swiglu_half_step(x3, g_f2_ref, wgu_ref, wo_ref, act_ref)


def _tail(x1, attn, yg, seq, g_fo, w_out, g_ca, w_cq, g_cq, kcT, vc, w_co, g_f2, wgu, wo):
    t, d = x1.shape
    d_ff = wo.shape[0]
    m = vc.shape[1]
    bps = seq // TM

    def rows(width):
        return pl.BlockSpec((TM, width), lambda r: (r, 0))

    return pl.pallas_call(
        _tail_kernel,
        grid=(t // TM,),
        in_specs=[rows(d), rows(FOX_W), rows(GMLP_W), _resident((1, FOX_W)), _resident((d, d)),
                  _resident((1, d)), _resident((d, d)), _resident((1, d // CA_HEADS)),
                  pl.BlockSpec((1, d, m), lambda r: (r // bps, 0, 0)),
                  pl.BlockSpec((1, m, d), lambda r: (r // bps, 0, 0)),
                  _resident((d, d)), _resident((1, d)), _resident((d, 2 * d_ff)), _resident((d_ff, d))],
        out_specs=rows(d),
        out_shape=jax.ShapeDtypeStruct((t, d), F32),
        scratch_shapes=[pltpu.VMEM((TM, d_ff), BF16)],
        compiler_params=_params(1),
        name="tail",
    )(x1, attn, yg, g_fo, w_out, g_ca, w_cq, g_cq, kcT, vc, w_co, g_f2, wgu, wo)


def _placement():
    pm = np.zeros((PAIRS * BIAS_ROWS, 32), np.float32)
    for p in range(PAIRS):
        for e in range(2):
            for part in range(3):
                pm[BIAS_ROWS * p + 3 * e + part, 8 * part + 2 * p + e] = -1.0
        pm[BIAS_ROWS * p + 6:BIAS_ROWS * p + 9, 24] = 1.0
    return pm


def _layer(x, mem, g_ffn1, w_ffn1_in, w_ffn1_out, g_mix, w_in, b_f, g_q, g_k, g_sgu, w_s, b_s, g_fox_o,
           g_gmlp_o, w_out, g_ca, g_mem, w_cq, w_ckv, g_cq, g_ck, w_co, g_ffn2, w_ffn2_in, w_ffn2_out):
    batch, seq, d = x.shape
    t = batch * seq
    m = mem.shape[1]
    dh_ca = d // CA_HEADS

    def row(v):
        return v.reshape(1, -1).astype(F32)

    k_off, v_off, f_off, uv_off = FOX_W, 2 * FOX_W, 3 * FOX_W, 3 * FOX_W + FOX_HEADS
    w_nn = jnp.concatenate([w_in[:, k_off:v_off], w_in[:, uv_off:]], axis=1).astype(BF16)
    w_nt = jnp.concatenate([w_in[:, :k_off].T, w_in[:, v_off:f_off].T, w_in[:, f_off:uv_off].T,
                            jnp.zeros((8, d), F32)], axis=0).astype(BF16)
    gq = jnp.broadcast_to((g_q * (FOX_DH ** -0.5 * LOG2E))[:, None], (FOX_DH, TM)).astype(F32)
    gk = row(jnp.tile(g_k, FOX_HEADS))
    bfc = jnp.broadcast_to(b_f[:, None], (FOX_HEADS, TM)).astype(F32)
    headsum = jnp.asarray(np.kron(np.eye(FOX_HEADS, dtype=np.float32),
                                  np.ones((FOX_DH, FOX_DH), np.float32)), BF16)
    triu = jnp.asarray(np.triu(np.ones((TM, TM), np.float32)), BF16)
    place = jnp.asarray(_placement(), BF16)
    bs = jnp.repeat(b_s.T, GMLP_DG, axis=1).astype(F32)
    g_cq_s = row(g_cq * (dh_ca ** -0.5 * LOG2E))
    g_ck_col = jnp.broadcast_to(g_ck[:, None], (dh_ca, m)).astype(F32)

    kcT, vc = _memkv(mem, row(g_mem), w_ckv[:, :d].T.astype(BF16), w_ckv[:, d:].astype(BF16), g_ck_col)

    x1 = _ffn(x.reshape(t, d), row(g_ffn1), w_ffn1_in.astype(BF16), w_ffn1_out.astype(BF16))
    qT, cT, k, cb, vT, yg = _proj(x1, batch, seq, row(g_mix), w_nn, w_nt, gq, gk, bfc, headsum, triu, place,
                                  row(g_sgu), w_s.astype(BF16), bs, row(g_gmlp_o))
    attn = _fox(qT, cT, k.reshape(batch, seq, FOX_W), cb.reshape(batch, seq, LANES), vT)
    out = _tail(x1, attn.reshape(t, FOX_W), yg, seq, row(g_fox_o), w_out.astype(BF16), row(g_ca),
                w_cq.astype(BF16), g_cq_s, kcT, vc, w_co.astype(BF16), row(g_ffn2),
                w_ffn2_in.astype(BF16), w_ffn2_out.astype(BF16))
    return out.reshape(batch, seq, d)


def kernel(x, mem, g_ffn1, w_ffn1_in, w_ffn1_out, g_mix, w_in, b_f, g_q, g_k, g_sgu, w_s, b_s, g_fox_o,
           g_gmlp_o, w_out, g_ca, g_mem, w_cq, w_ckv, g_cq, g_ck, w_co, g_ffn2, w_ffn2_in, w_ffn2_out):
    layer_params = (g_ffn1, w_ffn1_in, w_ffn1_out, g_mix, w_in, b_f, g_q, g_k, g_sgu, w_s, b_s, g_fox_o,
                    g_gmlp_o, w_out, g_ca, g_mem, w_cq, w_ckv, g_cq, g_ck, w_co, g_ffn2, w_ffn2_in, w_ffn2_out)
    for l in range(g_ffn1.shape[0]):
        x = _layer(x, mem, *(p[l] for p in layer_params))
    return x
```

```python
import functools

import jax
import jax.numpy as jnp
import numpy as np
from jax import lax
from jax.experimental import pallas as pl
from jax.experimental.pallas import tpu as pltpu

F32 = jnp.float32
BF16 = jnp.bfloat16
EPS = 1e-6
LOG2E = 1.4426950408889634
MASKED = -1e30

FOX_HEADS = 8
FOX_DH = 64
FOX_W = FOX_HEADS * FOX_DH
GMLP_GROUPS = 8
GMLP_DG = 64
GMLP_W = GMLP_GROUPS * GMLP_DG
CHUNK = 128
CA_HEADS = 4

LANES = 128
VMEM_LIMIT_BYTES = 56 * 1024 * 1024

TM = 512
TQ = 512
TK = 512
FC = 256
PAIRS = FOX_HEADS // 2
BIAS_ROWS = 32


def _rms(x):
    return x * lax.rsqrt(jnp.mean(x * x, axis=-1, keepdims=True) + EPS)


def _dot(a, b):
    return jnp.dot(a, b, preferred_element_type=F32)


def _dot_nt(a, b):
    return lax.dot_general(a, b, (((1,), (1,)), ((), ())), preferred_element_type=F32)


def _split3(x):
    hi = x.astype(BF16).astype(F32)
    mid = (x - hi).astype(BF16).astype(F32)
    lo = (x - hi - mid).astype(BF16).astype(F32)
    return hi, mid, lo


def _resident(shape):
    return pl.BlockSpec(shape, lambda *_: (0,) * len(shape), pipeline_mode=pl.Buffered(1))


def _params(n_axes):
    return pltpu.CompilerParams(dimension_semantics=("arbitrary",) * n_axes,
                                vmem_limit_bytes=VMEM_LIMIT_BYTES)


def _swiglu_half_step(x, g_ref, wgu_ref, wo_ref, act_ref):
    d_ff = wo_ref.shape[0]
    h = (_rms(x) * g_ref[...]).astype(BF16)
    for c in range(d_ff // FC):
        gate = _dot(h, wgu_ref[:, c * FC:(c + 1) * FC])
        up = _dot(h, wgu_ref[:, d_ff + c * FC:d_ff + (c + 1) * FC])
        act_ref[:, c * FC:(c + 1) * FC] = (gate * jax.nn.sigmoid(gate) * up).astype(BF16)
    return x + 0.5 * _dot(act_ref[...], wo_ref[...])


def _ffn_kernel(x_ref, g_ref, wgu_ref, wo_ref, o_ref, act_ref):
    o_ref[...] = _swiglu_half_step(x_ref[...], g_ref, wgu_ref, wo_ref, act_ref)


def _ffn(x, g, wgu, wo):
    t, d = x.shape
    d_ff = wo.shape[0]
    row = pl.BlockSpec((TM, d), lambda r: (r, 0))
    return pl.pallas_call(
        _ffn_kernel,
        grid=(t // TM,),
        in_specs=[row, _resident((1, d)), _resident((d, 2 * d_ff)), _resident((d_ff, d))],
        out_specs=row,
        out_shape=jax.ShapeDtypeStruct((t, d), F32),
        scratch_shapes=[pltpu.VMEM((TM, d_ff), BF16)],
        compiler_params=_params(1),
        name="ffn",
    )(x, g, wgu, wo)


def _gelu_tanh(x):
    return x * (0.5 * (1.0 + jnp.tanh(np.float32(np.sqrt(2.0 / np.pi)) * (x + 0.044715 * (x * x * x)))))


def _proj_kernel(x_ref, g_mix_ref, w_nn_ref, w_nt_ref, gq_ref, gk_ref, bf_ref, headsum_ref, triu_ref,
                 place_ref, g_sgu_ref, ws_ref, bs_ref, g_go_ref,
                 qT_ref, cT_ref, k_ref, cb_ref, vT_ref, yg_ref,
                 carry_ref, sgu_ref, *, blocks_per_seq):
    h = (_rms(x_ref[...]) * g_mix_ref[...]).astype(BF16)

    zT = _dot_nt(w_nt_ref[...], h)
    for hd in range(FOX_HEADS):
        blk = zT[hd * FOX_DH:(hd + 1) * FOX_DH, :]
        inv = lax.rsqrt(jnp.mean(blk * blk, axis=0, keepdims=True) + EPS)
        qT_ref[0, hd * FOX_DH:(hd + 1) * FOX_DH, :] = (blk * inv * gq_ref[...]).astype(BF16)
    for c in range(TM // TK):
        vT_ref[0, c] = zT[FOX_W:2 * FOX_W, c * TK:(c + 1) * TK].astype(BF16)

    @pl.when(pl.program_id(0) % blocks_per_seq == 0)
    def _():
        carry_ref[...] = jnp.zeros_like(carry_ref)

    f = zT[2 * FOX_W:2 * FOX_W + FOX_HEADS, :] + bf_ref[...]
    logf = (jnp.minimum(f, 0.0) - jnp.log1p(jnp.exp(-jnp.abs(f)))) * LOG2E
    zeros8 = jnp.zeros((8, TM), F32)
    parts = jnp.concatenate(_split3(logf) + (zeros8,), axis=0).astype(BF16)
    cs = _dot(parts, triu_ref[...])
    c = cs[0:8] + cs[8:16] + cs[16:24] + carry_ref[:, 0:1]
    carry_ref[...] = jnp.broadcast_to(c[:, TM - 1:TM], carry_ref.shape)
    cT_ref[0] = c

    one_row = jnp.where(lax.broadcasted_iota(jnp.int32, (8, TM), 0) == 0, 1.0, 0.0)
    cparts = jnp.concatenate(_split3(c) + (one_row,), axis=0).astype(BF16)
    cb_ref[...] = _dot(place_ref[...], cparts).T.astype(BF16)

    z = _dot(h, w_nn_ref[...])
    k = z[:, :FOX_W]
    ksum = _dot((k * k).astype(BF16), headsum_ref[...])
    k_ref[...] = (k * lax.rsqrt(ksum * (1.0 / FOX_DH) + EPS) * gk_ref[...]).astype(BF16)

    u = _gelu_tanh(z[:, FOX_W:FOX_W + GMLP_W])
    vg = _gelu_tanh(z[:, FOX_W + GMLP_W:])
    vgn = (_rms(vg) * g_sgu_ref[...]).astype(BF16)
    lane = lax.broadcasted_iota(jnp.int32, (CHUNK, LANES), 1)
    tril = (lax.broadcasted_iota(jnp.int32, (CHUNK, CHUNK), 0)
            >= lax.broadcasted_iota(jnp.int32, (CHUNK, CHUNK), 1))
    n_chunks = TM // CHUNK
    for p in range(GMLP_GROUPS // 2):
        lo_blocks, hi_blocks = [], []
        for ch in range(n_chunks):
            vp = vgn[ch * CHUNK:(ch + 1) * CHUNK, p * LANES:(p + 1) * LANES]
            lo_blocks.append(jnp.where(lane < GMLP_DG, vp, jnp.zeros_like(vp)))
            hi_blocks.append(jnp.where(lane >= GMLP_DG, vp, jnp.zeros_like(vp)))
        vcat = jnp.concatenate([jnp.concatenate(lo_blocks, axis=1),
                                jnp.concatenate(hi_blocks, axis=1)], axis=0)
        w0 = ws_ref[2 * p]
        w1 = ws_ref[2 * p + 1]
        wpair = jnp.concatenate([jnp.where(tril, w0, jnp.zeros_like(w0)),
                                 jnp.where(tril, w1, jnp.zeros_like(w1))], axis=1)
        mixed = _dot(wpair, vcat)
        bias = bs_ref[:, p * LANES:(p + 1) * LANES]
        for ch in range(n_chunks):
            sgu_ref[ch * CHUNK:(ch + 1) * CHUNK, p * LANES:(p + 1) * LANES] = (
                u[ch * CHUNK:(ch + 1) * CHUNK, p * LANES:(p + 1) * LANES]
                * (mixed[:, ch * LANES:(ch + 1) * LANES] + bias))
    yg_ref[...] = (_rms(sgu_ref[...]) * g_go_ref[...]).astype(BF16)


def _proj(x, batch, seq, g_mix, w_nn, w_nt, gq, gk, bfc, headsum, triu, place, g_sgu, ws, bs, g_go):
    t, d = x.shape
    bps = seq // TM
    n_nt = w_nt.shape[0]

    def rows(width):
        return pl.BlockSpec((TM, width), lambda r: (r, 0))

    kern = functools.partial(_proj_kernel, blocks_per_seq=bps)
    return pl.pallas_call(
        kern,
        grid=(t // TM,),
        in_specs=[rows(d), _resident((1, d)), _resident((d, w_nn.shape[1])), _resident((n_nt, d)),
                  _resident((FOX_DH, TM)), _resident((1, FOX_W)), _resident((FOX_HEADS, TM)),
                  _resident((FOX_W, FOX_W)), _resident((TM, TM)), _resident((PAIRS * BIAS_ROWS, 32)),
                  _resident((1, GMLP_W)), _resident((GMLP_GROUPS, CHUNK, CHUNK)),
                  _resident((CHUNK, GMLP_W)), _resident((1, GMLP_W))],
        out_specs=[pl.BlockSpec((1, FOX_W, TM), lambda r: (r // bps, 0, r % bps)),
                   pl.BlockSpec((1, FOX_HEADS, TM), lambda r: (r // bps, 0, r % bps)),
                   rows(FOX_W), rows(LANES),
                   pl.BlockSpec((1, TM // TK, FOX_W, TK), lambda r: (r // bps, r % bps, 0, 0)),
                   rows(GMLP_W)],
        out_shape=[jax.ShapeDtypeStruct((batch, FOX_W, seq), BF16),
                   jax.ShapeDtypeStruct((batch, FOX_HEADS, seq), F32),
                   jax.ShapeDtypeStruct((t, FOX_W), BF16),
                   jax.ShapeDtypeStruct((t, LANES), BF16),
                   jax.ShapeDtypeStruct((batch, seq // TK, FOX_W, TK), BF16),
                   jax.ShapeDtypeStruct((t, GMLP_W), BF16)],
        scratch_shapes=[pltpu.VMEM((FOX_HEADS, LANES), F32), pltpu.VMEM((TM, GMLP_W), F32)],
        compiler_params=_params(1),
        name="proj",
    )(x, g_mix, w_nn, w_nt, gq, gk, bfc, headsum, triu, place, g_sgu, ws, bs, g_go)


def _fox_kernel(qT_ref, cT_ref, k_ref, cb_ref, vT_ref, o_ref, qq_ref, s_ref, acc_ref):
    i = pl.program_id(1)

    row = lax.broadcasted_iota(jnp.int32, (LANES, TQ), 0)
    for p in range(PAIRS):
        qpair = qT_ref[0, p * LANES:(p + 1) * LANES, :]
        for e in range(2):
            hd = 2 * p + e
            own = (row >= FOX_DH * e) & (row < FOX_DH * (e + 1))
            qpart = jnp.where(own, qpair, jnp.zeros_like(qpair))
            hi, mid, lo = _split3(cT_ref[0, hd:hd + 1, :])
            base = BIAS_ROWS * p
            sel = (row >= base + 3 * e) & (row < base + 3 * e + 3)
            bias = jnp.where(sel, 1.0, 0.0)
            bias = jnp.where(row == base + 6, hi, bias)
            bias = jnp.where(row == base + 7, mid, bias)
            bias = jnp.where(row == base + 8, lo, bias)
            qq_ref[p, :, e * TQ:(e + 1) * TQ] = jnp.concatenate([qpart, bias.astype(BF16)], axis=0)

    acc_ref[...] = jnp.zeros_like(acc_ref)

    def scores(p, j, diagonal):
        k0 = pl.multiple_of(j * TK, TK)
        kk = jnp.concatenate([k_ref[0, pl.ds(k0, TK), p * LANES:(p + 1) * LANES],
                              cb_ref[0, pl.ds(k0, TK), :]], axis=1)
        s = _dot(kk, qq_ref[p])
        if diagonal:
            future = (lax.broadcasted_iota(jnp.int32, (TK, 2 * TQ), 0)
                      > lax.broadcasted_iota(jnp.int32, (TK, 2 * TQ), 1) % TQ)
            s = jnp.where(future, MASKED, s)
        s_ref[p] = s
        return jnp.max(s, axis=0, keepdims=True)

    def absorb(p, j, smax, m_old, l_old):
        m = jnp.maximum(m_old, smax)
        alpha = jnp.exp2(m_old - m)
        pexp = jnp.exp2(s_ref[p] - m)
        l = alpha * l_old + jnp.sum(pexp, axis=0, keepdims=True)
        pb = pexp.astype(BF16)
        for e in range(2):
            hd = 2 * p + e
            pv = _dot(vT_ref[0, j, hd * FOX_DH:(hd + 1) * FOX_DH, :], pb[:, e * TQ:(e + 1) * TQ])
            acc_ref[hd] = alpha[:, e * TQ:(e + 1) * TQ] * acc_ref[hd] + pv
        return m, l

    def chunk(j, nxt, carry, diagonal):
        smax0, ms, ls = carry
        new_m, new_l = [], []
        smax = smax0
        for p in range(PAIRS):
            nxt_smax = scores(p + 1, j, diagonal) if p + 1 < PAIRS else scores(0, nxt, False)
            m, l = absorb(p, j, smax, ms[p], ls[p])
            new_m.append(m)
            new_l.append(l)
            smax = nxt_smax
        return smax, tuple(new_m), tuple(new_l)

    init = (scores(0, i, True),
            tuple(jnp.full((1, 2 * TQ), MASKED, F32) for _ in range(PAIRS)),
            tuple(jnp.zeros((1, 2 * TQ), F32) for _ in range(PAIRS)))
    carry = chunk(i, 0, init, True)
    _, _, ls = lax.fori_loop(0, i, lambda j, cr: chunk(j, j + 1, cr, False), carry)

    outs = []
    for hd in range(FOX_HEADS):
        p, e = divmod(hd, 2)
        outs.append(acc_ref[hd] * (1.0 / ls[p][:, e * TQ:(e + 1) * TQ]))
    o_ref[0] = jnp.concatenate(outs, axis=0).T


def _fox(qT, cT, k, cb, vT):
    batch, _, seq = qT.shape
    return pl.pallas_call(
        _fox_kernel,
        grid=(batch, seq // TQ),
        in_specs=[pl.BlockSpec((1, FOX_W, TQ), lambda b, i: (b, 0, i)),
                  pl.BlockSpec((1, FOX_HEADS, TQ), lambda b, i: (b, 0, i)),
                  pl.BlockSpec((1, seq, FOX_W), lambda b, i: (b, 0, 0)),
                  pl.BlockSpec((1, seq, LANES), lambda b, i: (b, 0, 0)),
                  pl.BlockSpec((1, seq // TK, FOX_W, TK), lambda b, i: (b, 0, 0, 0))],
        out_specs=pl.BlockSpec((1, TQ, FOX_W), lambda b, i: (b, i, 0)),
        out_shape=jax.ShapeDtypeStruct((batch, seq, FOX_W), F32),
        scratch_shapes=[pltpu.VMEM((PAIRS, 2 * LANES, 2 * TQ), BF16),
                        pltpu.VMEM((PAIRS, TK, 2 * TQ), F32),
                        pltpu.VMEM((FOX_HEADS, FOX_DH, TQ), F32)],
        compiler_params=_params(2),
        name="fox",
    )(qT, cT, k, cb, vT)


def _memkv_kernel(mem_ref, g_mem_ref, w_ckT_ref, w_cv_ref, g_ck_ref, kcT_ref, vc_ref):
    mn = (_rms(mem_ref[0]) * g_mem_ref[...]).astype(BF16)
    kT = _dot_nt(w_ckT_ref[...], mn)
    dh = kT.shape[0] // CA_HEADS
    for hd in range(CA_HEADS):
        blk = kT[hd * dh:(hd + 1) * dh, :]
        inv = lax.rsqrt(jnp.mean(blk * blk, axis=0, keepdims=True) + EPS)
        kcT_ref[0, hd * dh:(hd + 1) * dh, :] = (blk * inv * g_ck_ref[...]).astype(BF16)
    vc_ref[0] = _dot(mn, w_cv_ref[...]).astype(BF16)


def _memkv(mem, g_mem, w_ckT, w_cv, g_ck_col):
    batch, m, d = mem.shape
    return pl.pallas_call(
        _memkv_kernel,
        grid=(batch,),
        in_specs=[pl.BlockSpec((1, m, d), lambda b: (b, 0, 0)), _resident((1, d)), _resident((d, d)),
                  _resident((d, d)), _resident((d // CA_HEADS, m))],
        out_specs=[pl.BlockSpec((1, d, m), lambda b: (b, 0, 0)), pl.BlockSpec((1, m, d), lambda b: (b, 0, 0))],
        out_shape=[jax.ShapeDtypeStruct((batch, d, m), BF16), jax.ShapeDtypeStruct((batch, m, d), BF16)],
        compiler_params=_params(1),
        name="memkv",
    )(mem, g_mem, w_ckT, w_cv, g_ck_col)


def _tail_kernel(x_ref, attn_ref, yg_ref, g_fo_ref, w_out_ref, g_ca_ref, w_cq_ref, g_cq_ref, kcT_ref, vc_ref,
                 w_co_ref, g_f2_ref, wgu_ref, wo_ref, o_ref, act_ref):
    a = (_rms(attn_ref[...]) * g_fo_ref[...]).astype(BF16)
    x2 = x_ref[...] + _dot(jnp.concatenate([a, yg_ref[...]], axis=1), w_out_ref[...])

    hq = (_rms(x2) * g_ca_ref[...]).astype(BF16)
    qc = _dot(hq, w_cq_ref[...])
    dh = qc.shape[1] // CA_HEADS
    outs = []
    for hd in range(CA_HEADS):
        qh = (_rms(qc[:, hd * dh:(hd + 1) * dh]) * g_cq_ref[...]).astype(BF16)
        s = _dot(qh, kcT_ref[0, hd * dh:(hd + 1) * dh, :])
        pexp = jnp.exp2(s - jnp.max(s, axis=-1, keepdims=True))
        denom = jnp.sum(pexp, axis=-1, keepdims=True)
        o = _dot(pexp.astype(BF16), vc_ref[0, :, hd * dh:(hd + 1) * dh]) * (1.0 / denom)
        outs.append(o.astype(BF16))
    x3 = x2 + _dot(jnp.concatenate(outs, axis=1), w_co_ref[...])

    o_ref[...] = _swiglu_half_step(x3, g_f2_ref, wgu_ref, wo_ref, act_ref)


def _tail(x1, attn, yg, seq, g_fo, w_out, g_ca, w_cq, g_cq, kcT, vc, w_co, g_f2, wgu, wo):
    t, d = x1.shape
    d_ff = wo.shape[0]
    m = vc.shape[1]
    bps = seq // TM

    def rows(width):
        return pl.BlockSpec((TM, width), lambda r: (r, 0))

    return pl.pallas_call(
        _tail_kernel,
        grid=(t // TM,),
        in_specs=[rows(d), rows(FOX_W), rows(GMLP_W), _resident((1, FOX_W)), _resident((d, d)),
                  _resident((1, d)), _resident((d, d)), _resident((1, d // CA_HEADS)),
                  pl.BlockSpec((1, d, m), lambda r: (r // bps, 0, 0)),
                  pl.BlockSpec((1, m, d), lambda r: (r // bps, 0, 0)),
                  _resident((d, d)), _resident((1, d)), _resident((d, 2 * d_ff)), _resident((d_ff, d))],
        out_specs=rows(d),
        out_shape=jax.ShapeDtypeStruct((t, d), F32),
        scratch_shapes=[pltpu.VMEM((TM, d_ff), BF16)],
        compiler_params=_params(1),
        name="tail",
    )(x1, attn, yg, g_fo, w_out, g_ca, w_cq, g_cq, kcT, vc, w_co, g_f2, wgu, wo)


def _placement():
    pm = np.zeros((PAIRS * BIAS_ROWS, 32), np.float32)
    for p in range(PAIRS):
        for e in range(2):
            for part in range(3):
                pm[BIAS_ROWS * p + 3 * e + part, 8 * part + 2 * p + e] = -1.0
        pm[BIAS_ROWS * p + 6:BIAS_ROWS * p + 9, 24] = 1.0
    return pm


def _layer(x, mem, g_ffn1, w_ffn1_in, w_ffn1_out, g_mix, w_in, b_f, g_q, g_k, g_sgu, w_s, b_s, g_fox_o,
           g_gmlp_o, w_out, g_ca, g_mem, w_cq, w_ckv, g_cq, g_ck, w_co, g_ffn2, w_ffn2_in, w_ffn2_out):
    batch, seq, d = x.shape
    t = batch * seq
    m = mem.shape[1]
    dh_ca = d // CA_HEADS

    def row(v):
        return v.reshape(1, -1).astype(F32)

    k_off, v_off, f_off, uv_off = FOX_W, 2 * FOX_W, 3 * FOX_W, 3 * FOX_W + FOX_HEADS
    w_nn = jnp.concatenate([w_in[:, k_off:v_off], w_in[:, uv_off:]], axis=1).astype(BF16)
    w_nt = jnp.concatenate([w_in[:, :k_off].T, w_in[:, v_off:f_off].T, w_in[:, f_off:uv_off].T,
                            jnp.zeros((8, d), F32)], axis=0).astype(BF16)
    gq = jnp.broadcast_to((g_q * (FOX_DH ** -0.5 * LOG2E))[:, None], (FOX_DH, TM)).astype(F32)
    gk = row(jnp.tile(g_k, FOX_HEADS))
    bfc = jnp.broadcast_to(b_f[:, None], (FOX_HEADS, TM)).astype(F32)
    headsum = jnp.asarray(np.kron(np.eye(FOX_HEADS, dtype=np.float32),
                                  np.ones((FOX_DH, FOX_DH), np.float32)), BF16)
    triu = jnp.asarray(np.triu(np.ones((TM, TM), np.float32)), BF16)
    place = jnp.asarray(_placement(), BF16)
    bs = jnp.repeat(b_s.T, GMLP_DG, axis=1).astype(F32)
    g_cq_s = row(g_cq * (dh_ca ** -0.5 * LOG2E))
    g_ck_col = jnp.broadcast_to(g_ck[:, None], (dh_ca, m)).astype(F32)

    kcT, vc = _memkv(mem, row(g_mem), w_ckv[:, :d].T.astype(BF16), w_ckv[:, d:].astype(BF16), g_ck_col)

    x1 = _ffn(x.reshape(t, d), row(g_ffn1), w_ffn1_in.astype(BF16), w_ffn1_out.astype(BF16))
    qT, cT, k, cb, vT, yg = _proj(x1, batch, seq, row(g_mix), w_nn, w_nt, gq, gk, bfc, headsum, triu, place,
                                  row(g_sgu), w_s.astype(BF16), bs, row(g_gmlp_o))
    attn = _fox(qT, cT, k.reshape(batch, seq, FOX_W), cb.reshape(batch, seq, LANES), vT)
    out = _tail(x1, attn.reshape(t, FOX_W), yg, seq, row(g_fox_o), w_out.astype(BF16), row(g_ca),
                w_cq.astype(BF16), g_cq_s, kcT, vc, w_co.astype(BF16), row(g_ffn2),
                w_ffn2_in.astype(BF16), w_ffn2_out.astype(BF16))
    return out.reshape(batch, seq, d)


def kernel(x, mem, g_ffn1, w_ffn1_in, w_ffn1_out, g_mix, w_in, b_f, g_q, g_k, g_sgu, w_s, b_s, g_fox_o,
           g_gmlp_o, w_out, g_ca, g_mem, w_cq, w_ckv, g_cq, g_ck, w_co, g_ffn2, w_ffn2_in, w_ffn2_out):
    layer_params = (g_ffn1, w_ffn1_in, w_ffn1_out, g_mix, w_in, b_f, g_q, g_k, g_sgu, w_s, b_s, g_fox_o,
                    g_gmlp_o, w_out, g_ca, g_mem, w_cq, w_ckv, g_cq, g_ck, w_co, g_ffn2, w_ffn2_in, w_ffn2_out)
    for l in range(g_ffn1.shape[0]):
        x = _layer(x, mem, *(p[l] for p in layer_params))
    return x
```

```python
import functools

import jax
import jax.numpy as jnp
import numpy as np
from jax import lax
from jax.experimental import pallas as pl
from jax.experimental.pallas import tpu as pltpu

F32 = jnp.float32
BF16 = jnp.bfloat16
EPS = 1e-6
LOG2E = 1.4426950408889634
MASKED = -1e30

FOX_HEADS = 8
FOX_DH = 64
FOX_W = FOX_HEADS * FOX_DH
GMLP_GROUPS = 8
GMLP_DG = 64
GMLP_W = GMLP_GROUPS * GMLP_DG
CHUNK = 128
CA_HEADS = 4

LANES = 128
VMEM_LIMIT_BYTES = 56 * 1024 * 1024

TM = 512
TQ = 512
TK = 512
FC = 256
PAIRS = FOX_HEADS // 2
BIAS_ROWS = 32
V_ROWS = FOX_DH + 16


def _rms(x):
    return x * lax.rsqrt(jnp.mean(x * x, axis=-1, keepdims=True) + EPS)


def _dot(a, b):
    return jnp.dot(a, b, preferred_element_type=F32)


def _dot_nt(a, b):
    return lax.dot_general(a, b, (((1,), (1,)), ((), ())), preferred_element_type=F32)


def _split3(x):
    hi = x.astype(BF16).astype(F32)
    mid = (x - hi).astype(BF16).astype(F32)
    lo = (x - hi - mid).astype(BF16).astype(F32)
    return hi, mid, lo


def _resident(shape):
    return pl.BlockSpec(shape, lambda *_: (0,) * len(shape), pipeline_mode=pl.Buffered(1))


def _params(n_axes):
    return pltpu.CompilerParams(dimension_semantics=("arbitrary",) * n_axes,
                                vmem_limit_bytes=VMEM_LIMIT_BYTES)


def _swiglu_half_step(x, g_ref, wgu_ref, wo_ref, act_ref):
    d_ff = wo_ref.shape[0]
    h = (_rms(x) * g_ref[...]).astype(BF16)
    for c in range(d_ff // FC):
        gate = _dot(h, wgu_ref[:, c * FC:(c + 1) * FC])
        up = _dot(h, wgu_ref[:, d_ff + c * FC:d_ff + (c + 1) * FC])
        act_ref[:, c * FC:(c + 1) * FC] = (gate * jax.nn.sigmoid(gate) * up).astype(BF16)
    return x + 0.5 * _dot(act_ref[...], wo_ref[...])


def _ffn_kernel(x_ref, g_ref, wgu_ref, wo_ref, o_ref, act_ref):
    o_ref[...] = _swiglu_half_step(x_ref[...], g_ref, wgu_ref, wo_ref, act_ref)


def _ffn(x, g, wgu, wo):
    t, d = x.shape
    d_ff = wo.shape[0]
    row = pl.BlockSpec((TM, d), lambda r: (r, 0))
    return pl.pallas_call(
        _ffn_kernel,
        grid=(t // TM,),
        in_specs=[row, _resident((1, d)), _resident((d, 2 * d_ff)), _resident((d_ff, d))],
        out_specs=row,
        out_shape=jax.ShapeDtypeStruct((t, d), F32),
        scratch_shapes=[pltpu.VMEM((TM, d_ff), BF16)],
        compiler_params=_params(1),
        name="ffn",
    )(x, g, wgu, wo)


def _gelu_tanh(x):
    return x * (0.5 * (1.0 + jnp.tanh(np.float32(np.sqrt(2.0 / np.pi)) * (x + 0.044715 * (x * x * x)))))


def _proj_kernel(x_ref, g_mix_ref, w_nn_ref, w_nt_ref, gq_ref, gk_ref, bf_ref, headsum_ref, triu_ref,
                 place_ref, g_sgu_ref, ws_ref, bs_ref, g_go_ref,
                 qT_ref, cT_ref, k_ref, cb_ref, vT_ref, yg_ref,
                 carry_ref, sgu_ref, *, blocks_per_seq):
    h = (_rms(x_ref[...]) * g_mix_ref[...]).astype(BF16)

    zT = _dot_nt(w_nt_ref[...], h)
    for hd in range(FOX_HEADS):
        blk = zT[hd * FOX_DH:(hd + 1) * FOX_DH, :]
        inv = lax.rsqrt(jnp.mean(blk * blk, axis=0, keepdims=True) + EPS)
        qT_ref[0, hd * FOX_DH:(hd + 1) * FOX_DH, :] = (blk * inv * gq_ref[...]).astype(BF16)
    ones_rows = jnp.where(lax.broadcasted_iota(jnp.int32, (V_ROWS - FOX_DH, TK), 0) == 0, 1.0, 0.0).astype(BF16)
    for c in range(TM // TK):
        for hd in range(FOX_HEADS):
            vT_ref[0, c, hd * V_ROWS:hd * V_ROWS + FOX_DH, :] = (
                zT[FOX_W + hd * FOX_DH:FOX_W + (hd + 1) * FOX_DH, c * TK:(c + 1) * TK].astype(BF16))
            vT_ref[0, c, hd * V_ROWS + FOX_DH:(hd + 1) * V_ROWS, :] = ones_rows

    @pl.when(pl.program_id(0) % blocks_per_seq == 0)
    def _():
        carry_ref[...] = jnp.zeros_like(carry_ref)

    f = zT[2 * FOX_W:2 * FOX_W + FOX_HEADS, :] + bf_ref[...]
    logf = (jnp.minimum(f, 0.0) - jnp.log1p(jnp.exp(-jnp.abs(f)))) * LOG2E
    zeros8 = jnp.zeros((8, TM), F32)
    parts = jnp.concatenate(_split3(logf) + (zeros8,), axis=0).astype(BF16)
    cs = _dot(parts, triu_ref[...])
    c = cs[0:8] + cs[8:16] + cs[16:24] + carry_ref[:, 0:1]
    carry_ref[...] = jnp.broadcast_to(c[:, TM - 1:TM], carry_ref.shape)
    cT_ref[0] = c

    one_row = jnp.where(lax.broadcasted_iota(jnp.int32, (8, TM), 0) == 0, 1.0, 0.0)
    cparts = jnp.concatenate(_split3(c) + (one_row,), axis=0).astype(BF16)
    cb_ref[...] = _dot(place_ref[...], cparts).T.astype(BF16)

    z = _dot(h, w_nn_ref[...])
    k = z[:, :FOX_W]
    ksum = _dot((k * k).astype(BF16), headsum_ref[...])
    k_ref[...] = (k * lax.rsqrt(ksum * (1.0 / FOX_DH) + EPS) * gk_ref[...]).astype(BF16)

    u = _gelu_tanh(z[:, FOX_W:FOX_W + GMLP_W])
    vg = _gelu_tanh(z[:, FOX_W + GMLP_W:])
    vgn = (_rms(vg) * g_sgu_ref[...]).astype(BF16)
    lane = lax.broadcasted_iota(jnp.int32, (CHUNK, LANES), 1)
    tril = (lax.broadcasted_iota(jnp.int32, (CHUNK, CHUNK), 0)
            >= lax.broadcasted_iota(jnp.int32, (CHUNK, CHUNK), 1))
    n_chunks = TM // CHUNK
    for p in range(GMLP_GROUPS // 2):
        lo_blocks, hi_blocks = [], []
        for ch in range(n_chunks):
            vp = vgn[ch * CHUNK:(ch + 1) * CHUNK, p * LANES:(p + 1) * LANES]
            lo_blocks.append(jnp.where(lane < GMLP_DG, vp, jnp.zeros_like(vp)))
            hi_blocks.append(jnp.where(lane >= GMLP_DG, vp, jnp.zeros_like(vp)))
        vcat = jnp.concatenate([jnp.concatenate(lo_blocks, axis=1),
                                jnp.concatenate(hi_blocks, axis=1)], axis=0)
        w0 = ws_ref[2 * p]
        w1 = ws_ref[2 * p + 1]
        wpair = jnp.concatenate([jnp.where(tril, w0, jnp.zeros_like(w0)),
                                 jnp.where(tril, w1, jnp.zeros_like(w1))], axis=1)
        mixed = _dot(wpair, vcat)
        bias = bs_ref[:, p * LANES:(p + 1) * LANES]
        for ch in range(n_chunks):
            sgu_ref[ch * CHUNK:(ch + 1) * CHUNK, p * LANES:(p + 1) * LANES] = (
                u[ch * CHUNK:(ch + 1) * CHUNK, p * LANES:(p + 1) * LANES]
                * (mixed[:, ch * LANES:(ch + 1) * LANES] + bias))
    yg_ref[...] = (_rms(sgu_ref[...]) * g_go_ref[...]).astype(BF16)


def _proj(x, batch, seq, g_mix, w_nn, w_nt, gq, gk, bfc, headsum, triu, place, g_sgu, ws, bs, g_go):
    t, d = x.shape
    bps = seq // TM
    n_nt = w_nt.shape[0]

    def rows(width):
        return pl.BlockSpec((TM, width), lambda r: (r, 0))

    kern = functools.partial(_proj_kernel, blocks_per_seq=bps)
    return pl.pallas_call(
        kern,
        grid=(t // TM,),
        in_specs=[rows(d), _resident((1, d)), _resident((d, w_nn.shape[1])), _resident((n_nt, d)),
                  _resident((FOX_DH, TM)), _resident((1, FOX_W)), _resident((FOX_HEADS, TM)),
                  _resident((FOX_W, FOX_W)), _resident((TM, TM)), _resident((PAIRS * BIAS_ROWS, 32)),
                  _resident((1, GMLP_W)), _resident((GMLP_GROUPS, CHUNK, CHUNK)),
                  _resident((CHUNK, GMLP_W)), _resident((1, GMLP_W))],
        out_specs=[pl.BlockSpec((1, FOX_W, TM), lambda r: (r // bps, 0, r % bps)),
                   pl.BlockSpec((1, FOX_HEADS, TM), lambda r: (r // bps, 0, r % bps)),
                   rows(FOX_W), rows(LANES),
                   pl.BlockSpec((1, TM // TK, FOX_HEADS * V_ROWS, TK), lambda r: (r // bps, r % bps, 0, 0)),
                   rows(GMLP_W)],
        out_shape=[jax.ShapeDtypeStruct((batch, FOX_W, seq), BF16),
                   jax.ShapeDtypeStruct((batch, FOX_HEADS, seq), F32),
                   jax.ShapeDtypeStruct((t, FOX_W), BF16),
                   jax.ShapeDtypeStruct((t, LANES), BF16),
                   jax.ShapeDtypeStruct((batch, seq // TK, FOX_HEADS * V_ROWS, TK), BF16),
                   jax.ShapeDtypeStruct((t, GMLP_W), BF16)],
        scratch_shapes=[pltpu.VMEM((FOX_HEADS, LANES), F32), pltpu.VMEM((TM, GMLP_W), F32)],
        compiler_params=_params(1),
        name="proj",
    )(x, g_mix, w_nn, w_nt, gq, gk, bfc, headsum, triu, place, g_sgu, ws, bs, g_go)


def _fox_kernel(qT_ref, cT_ref, k_ref, cb_ref, vT_ref, o_ref, qq_ref, s_ref, acc_ref):
    i = pl.program_id(1)

    row = lax.broadcasted_iota(jnp.int32, (LANES, TQ), 0)
    for p in range(PAIRS):
        qpair = qT_ref[0, p * LANES:(p + 1) * LANES, :]
        for e in range(2):
            hd = 2 * p + e
            own = (row >= FOX_DH * e) & (row < FOX_DH * (e + 1))
            qpart = jnp.where(own, qpair, jnp.zeros_like(qpair))
            hi, mid, lo = _split3(cT_ref[0, hd:hd + 1, :])
            base = BIAS_ROWS * p
            sel = (row >= base + 3 * e) & (row < base + 3 * e + 3)
            bias = jnp.where(sel, 1.0, 0.0)
            bias = jnp.where(row == base + 6, hi, bias)
            bias = jnp.where(row == base + 7, mid, bias)
            bias = jnp.where(row == base + 8, lo, bias)
            qq_ref[p, :, e * TQ:(e + 1) * TQ] = jnp.concatenate([qpart, bias.astype(BF16)], axis=0)

    acc_ref[...] = jnp.zeros_like(acc_ref)

    def scores(p, j, diagonal):
        k0 = pl.multiple_of(j * TK, TK)
        kk = jnp.concatenate([k_ref[0, pl.ds(k0, TK), p * LANES:(p + 1) * LANES],
                              cb_ref[0, pl.ds(k0, TK), :]], axis=1)
        s = _dot(kk, qq_ref[p])
        if diagonal:
            future = (lax.broadcasted_iota(jnp.int32, (TK, 2 * TQ), 0)
                      > lax.broadcasted_iota(jnp.int32, (TK, 2 * TQ), 1) % TQ)
            s = jnp.where(future, MASKED, s)
        s_ref[p] = s
        return jnp.max(s, axis=0, keepdims=True)

    def absorb(p, j, smax, m_old):
        m = jnp.maximum(m_old, smax)
        alpha = jnp.exp2(m_old - m)
        pb = jnp.exp2(s_ref[p] - m).astype(BF16)
        for e in range(2):
            hd = 2 * p + e
            pv = _dot(vT_ref[0, j, hd * V_ROWS:(hd + 1) * V_ROWS, :], pb[:, e * TQ:(e + 1) * TQ])
            acc_ref[hd] = alpha[:, e * TQ:(e + 1) * TQ] * acc_ref[hd] + pv
        return m

    def chunk(j, nxt, carry, diagonal):
        smax, ms = carry
        new_m = []
        for p in range(PAIRS):
            nxt_smax = scores(p + 1, j, diagonal) if p + 1 < PAIRS else scores(0, nxt, False)
            new_m.append(absorb(p, j, smax, ms[p]))
            smax = nxt_smax
        return smax, tuple(new_m)

    init = (scores(0, i, True), tuple(jnp.full((1, 2 * TQ), MASKED, F32) for _ in range(PAIRS)))
    carry = chunk(i, 0, init, True)
    lax.fori_loop(0, i, lambda j, cr: chunk(j, j + 1, cr, False), carry)

    outs = []
    for hd in range(FOX_HEADS):
        acc = acc_ref[hd]
        outs.append(acc[:FOX_DH] * (1.0 / acc[FOX_DH:FOX_DH + 1]))
    o_ref[0] = jnp.concatenate(outs, axis=0).T


def _fox(qT, cT, k, cb, vT):
    batch, _, seq = qT.shape
    return pl.pallas_call(
        _fox_kernel,
        grid=(batch, seq // TQ),
        in_specs=[pl.BlockSpec((1, FOX_W, TQ), lambda b, i: (b, 0, i)),
                  pl.BlockSpec((1, FOX_HEADS, TQ), lambda b, i: (b, 0, i)),
                  pl.BlockSpec((1, seq, FOX_W), lambda b, i: (b, 0, 0)),
                  pl.BlockSpec((1, seq, LANES), lambda b, i: (b, 0, 0)),
                  pl.BlockSpec((1, seq // TK, FOX_HEADS * V_ROWS, TK), lambda b, i: (b, 0, 0, 0))],
        out_specs=pl.BlockSpec((1, TQ, FOX_W), lambda b, i: (b, i, 0)),
        out_shape=jax.ShapeDtypeStruct((batch, seq, FOX_W), F32),
        scratch_shapes=[pltpu.VMEM((PAIRS, 2 * LANES, 2 * TQ), BF16),
                        pltpu.VMEM((PAIRS, TK, 2 * TQ), F32),
                        pltpu.VMEM((FOX_HEADS, V_ROWS, TQ), F32)],
        compiler_params=_params(2),
        name="fox",
    )(qT, cT, k, cb, vT)


def _memkv_kernel(mem_ref, g_mem_ref, w_ckT_ref, w_cv_ref, g_ck_ref, kcT_ref, vc_ref):
    mn = (_rms(mem_ref[0]) * g_mem_ref[...]).astype(BF16)
    kT = _dot_nt(w_ckT_ref[...], mn)
    dh = kT.shape[0] // CA_HEADS
    for hd in range(CA_HEADS):
        blk = kT[hd * dh:(hd + 1) * dh, :]
        inv = lax.rsqrt(jnp.mean(blk * blk, axis=0, keepdims=True) + EPS)
        kcT_ref[0, hd * dh:(hd + 1) * dh, :] = (blk * inv * g_ck_ref[...]).astype(BF16)
    vc_ref[0] = _dot(mn, w_cv_ref[...]).astype(BF16)


def _memkv(mem, g_mem, w_ckT, w_cv, g_ck_col):
    batch, m, d = mem.shape
    return pl.pallas_call(
        _memkv_kernel,
        grid=(batch,),
        in_specs=[pl.BlockSpec((1, m, d), lambda b: (b, 0, 0)), _resident((1, d)), _resident((d, d)),
                  _resident((d, d)), _resident((d // CA_HEADS, m))],
        out_specs=[pl.BlockSpec((1, d, m), lambda b: (b, 0, 0)), pl.BlockSpec((1, m, d), lambda b: (b, 0, 0))],
        out_shape=[jax.ShapeDtypeStruct((batch, d, m), BF16), jax.ShapeDtypeStruct((batch, m, d), BF16)],
        compiler_params=_params(1),
        name="memkv",
    )(mem, g_mem, w_ckT, w_cv, g_ck_col)


def _tail_kernel(x_ref, attn_ref, yg_ref, g_fo_ref, w_out_ref, g_ca_ref, w_cq_ref, g_cq_ref, kcT_ref, vc_ref,
                 w_co_ref, g_f2_ref, wgu_ref, wo_ref, o_ref, act_ref):
    a = (_rms(attn_ref[...]) * g_fo_ref[...]).astype(BF16)
    x2 = x_ref[...] + _dot(jnp.concatenate([a, yg_ref[...]], axis=1), w_out_ref[...])

    hq = (_rms(x2) * g_ca_ref[...]).astype(BF16)
    qc = _dot(hq, w_cq_ref[...])
    dh = qc.shape[1] // CA_HEADS
    outs = []
    for hd in range(CA_HEADS):
        qh = (_rms(qc[:, hd * dh:(hd + 1) * dh]) * g_cq_ref[...]).astype(BF16)
        s = _dot(qh, kcT_ref[0, hd * dh:(hd + 1) * dh, :])
        pexp = jnp.exp2(s - jnp.max(s, axis=-1, keepdims=True))
        denom = jnp.sum(pexp, axis=-1, keepdims=True)
        o = _dot(pexp.astype(BF16), vc_ref[0, :, hd * dh:(hd + 1) * dh]) * (1.0 / denom)
        outs.append(o.astype(BF16))
    x3 = x2 + _dot(jnp.concatenate(outs, axis=1), w_co_ref[...])

    o_ref[...] = _swiglu_half_step(x3, g_f2_ref, wgu_ref, wo_ref, act_ref)


def _tail(x1, attn, yg, seq, g_fo, w_out, g_ca, w_cq, g_cq, kcT, vc, w_co, g_f2, wgu, wo):
    t, d = x1.shape
    d_ff = wo.shape[0]
    m = vc.shape[1]
    bps = seq // TM

    def rows(width):
        return pl.BlockSpec((TM, width), lambda r: (r, 0))

    return pl.pallas_call(
        _tail_kernel,
        grid=(t // TM,),
        in_specs=[rows(d), rows(FOX_W), rows(GMLP_W), _resident((1, FOX_W)), _resident((d, d)),
                  _resident((1, d)), _resident((d, d)), _resident((1, d // CA_HEADS)),
                  pl.BlockSpec((1, d, m), lambda r: (r // bps, 0, 0)),
                  pl.BlockSpec((1, m, d), lambda r: (r // bps, 0, 0)),
                  _resident((d, d)), _resident((1, d)), _resident((d, 2 * d_ff)), _resident((d_ff, d))],
        out_specs=rows(d),
        out_shape=jax.ShapeDtypeStruct((t, d), F32),
        scratch_shapes=[pltpu.VMEM((TM, d_ff), BF16)],
        compiler_params=_params(1),
        name="tail",
    )(x1, attn, yg, g_fo, w_out, g_ca, w_cq, g_cq, kcT, vc, w_co, g_f2, wgu, wo)


def _placement():
    pm = np.zeros((PAIRS * BIAS_ROWS, 32), np.float32)
    for p in range(PAIRS):
        for e in range(2):
            for part in range(3):
                pm[BIAS_ROWS * p + 3 * e + part, 8 * part + 2 * p + e] = -1.0
        pm[BIAS_ROWS * p + 6:BIAS_ROWS * p + 9, 24] = 1.0
    return pm


def _layer(x, mem, g_ffn1, w_ffn1_in, w_ffn1_out, g_mix, w_in, b_f, g_q, g_k, g_sgu, w_s, b_s, g_fox_o,
           g_gmlp_o, w_out, g_ca, g_mem, w_cq, w_ckv, g_cq, g_ck, w_co, g_ffn2, w_ffn2_in, w_ffn2_out):
    batch, seq, d = x.shape
    t = batch * seq
    m = mem.shape[1]
    dh_ca = d // CA_HEADS

    def row(v):
        return v.reshape(1, -1).astype(F32)

    k_off, v_off, f_off, uv_off = FOX_W, 2 * FOX_W, 3 * FOX_W, 3 * FOX_W + FOX_HEADS
    w_nn = jnp.concatenate([w_in[:, k_off:v_off], w_in[:, uv_off:]], axis=1).astype(BF16)
    w_nt = jnp.concatenate([w_in[:, :k_off].T, w_in[:, v_off:f_off].T, w_in[:, f_off:uv_off].T,
                            jnp.zeros((8, d), F32)], axis=0).astype(BF16)
    gq = jnp.broadcast_to((g_q * (FOX_DH ** -0.5 * LOG2E))[:, None], (FOX_DH, TM)).astype(F32)
    gk = row(jnp.tile(g_k, FOX_HEADS))
    bfc = jnp.broadcast_to(b_f[:, None], (FOX_HEADS, TM)).astype(F32)
    headsum = jnp.asarray(np.kron(np.eye(FOX_HEADS, dtype=np.float32),
                                  np.ones((FOX_DH, FOX_DH), np.float32)), BF16)
    triu = jnp.asarray(np.triu(np.ones((TM, TM), np.float32)), BF16)
    place = jnp.asarray(_placement(), BF16)
    bs = jnp.repeat(b_s.T, GMLP_DG, axis=1).astype(F32)
    g_cq_s = row(g_cq * (dh_ca ** -0.5 * LOG2E))
    g_ck_col = jnp.broadcast_to(g_ck[:, None], (dh_ca, m)).astype(F32)

    kcT, vc = _memkv(mem, row(g_mem), w_ckv[:, :d].T.astype(BF16), w_ckv[:, d:].astype(BF16), g_ck_col)

    x1 = _ffn(x.reshape(t, d), row(g_ffn1), w_ffn1_in.astype(BF16), w_ffn1_out.astype(BF16))
    qT, cT, k, cb, vT, yg = _proj(x1, batch, seq, row(g_mix), w_nn, w_nt, gq, gk, bfc, headsum, triu, place,
                                  row(g_sgu), w_s.astype(BF16), bs, row(g_gmlp_o))
    attn = _fox(qT, cT, k.reshape(batch, seq, FOX_W), cb.reshape(batch, seq, LANES), vT)
    out = _tail(x1, attn.reshape(t, FOX_W), yg, seq, row(g_fox_o), w_out.astype(BF16), row(g_ca),
                w_cq.astype(BF16), g_cq_s, kcT, vc, w_co.astype(BF16), row(g_ffn2),
                w_ffn2_in.astype(BF16), w_ffn2_out.astype(BF16))
    return out.reshape(batch, seq, d)


def kernel(x, mem, g_ffn1, w_ffn1_in, w_ffn1_out, g_mix, w_in, b_f, g_q, g_k, g_sgu, w_s, b_s, g_fox_o,
           g_gmlp_o, w_out, g_ca, g_mem, w_cq, w_ckv, g_cq, g_ck, w_co, g_ffn2, w_ffn2_in, w_ffn2_out):
    layer_params = (g_ffn1, w_ffn1_in, w_ffn1_out, g_mix, w_in, b_f, g_q, g_k, g_sgu, w_s, b_s, g_fox_o,
                    g_gmlp_o, w_out, g_ca, g_mem, w_cq, w_ckv, g_cq, g_ck, w_co, g_ffn2, w_ffn2_in, w_ffn2_out)
    for l in range(g_ffn1.shape[0]):
        x = _layer(x, mem, *(p[l] for p in layer_params))
    return x
```

```python
import functools

import jax
import jax.numpy as jnp
import numpy as np
from jax import lax
from jax.experimental import pallas as pl
from jax.experimental.pallas import tpu as pltpu

F32 = jnp.float32
BF16 = jnp.bfloat16
EPS = 1e-6
LOG2E = 1.4426950408889634
MASKED = -1e30

FOX_HEADS = 8
FOX_DH = 64
FOX_W = FOX_HEADS * FOX_DH
GMLP_GROUPS = 8
GMLP_DG = 64
GMLP_W = GMLP_GROUPS * GMLP_DG
CHUNK = 128
CA_HEADS = 4

LANES = 128
VMEM_LIMIT_BYTES = 56 * 1024 * 1024

TM = 512
TQ = 512
TK = 512
FC = 256
PAIRS = FOX_HEADS // 2
BIAS_ROWS = 32
V_ROWS = FOX_DH + 16


def _rms(x):
    return x * lax.rsqrt(jnp.mean(x * x, axis=-1, keepdims=True) + EPS)


def _dot(a, b):
    return lax.dot_general(a, b, (((1,), (0,)), ((), ())), preferred_element_type=F32)


def _dot_nt(a, b):
    return lax.dot_general(a, b, (((1,), (1,)), ((), ())), preferred_element_type=F32)


def _split3(x):
    hi = x.astype(BF16).astype(F32)
    mid = (x - hi).astype(BF16).astype(F32)
    lo = (x - hi - mid).astype(BF16).astype(F32)
    return hi, mid, lo


def _resident(shape):
    return pl.BlockSpec(shape, lambda *_: (0,) * len(shape), pipeline_mode=pl.Buffered(1))


def _params(n_axes):
    return pltpu.CompilerParams(dimension_semantics=("arbitrary",) * n_axes,
                                vmem_limit_bytes=VMEM_LIMIT_BYTES)


def _swiglu_half_step(x, g_ref, wgu_ref, wo_ref, act_ref):
    d_ff = wo_ref.shape[0]
    h = (_rms(x) * g_ref[...]).astype(BF16)
    for c in range(d_ff // FC):
        gate = _dot(h, wgu_ref[:, c * FC:(c + 1) * FC])
        up = _dot(h, wgu_ref[:, d_ff + c * FC:d_ff + (c + 1) * FC])
        act_ref[:, c * FC:(c + 1) * FC] = (gate * jax.nn.sigmoid(gate) * up).astype(BF16)
    return x + 0.5 * _dot(act_ref[...], wo_ref[...])


def _ffn_kernel(x_ref, g_ref, wgu_ref, wo_ref, o_ref, act_ref):
    o_ref[...] = _swiglu_half_step(x_ref[...], g_ref, wgu_ref, wo_ref, act_ref)


def _ffn(x, g, wgu, wo):
    t, d = x.shape
    d_ff = wo.shape[0]
    row = pl.BlockSpec((TM, d), lambda r: (r, 0))
    return pl.pallas_call(
        _ffn_kernel,
        grid=(t // TM,),
        in_specs=[row, _resident((1, d)), _resident((d, 2 * d_ff)), _resident((d_ff, d))],
        out_specs=row,
        out_shape=jax.ShapeDtypeStruct((t, d), F32),
        scratch_shapes=[pltpu.VMEM((TM, d_ff), BF16)],
        compiler_params=_params(1),
        name="ffn",
    )(x, g, wgu, wo)


def _gelu_tanh(x):
    c = float(np.sqrt(2.0 / np.pi))
    return x * (0.5 * jnp.tanh(x * (c + (c * 0.044715) * (x * x))) + 0.5)


def _proj_kernel(x_ref, g_mix_ref, w_nn_ref, w_nt_ref, gq_ref, gk_ref, bf_ref, headsum_ref, triu_ref,
                 place_ref, g_sgu_ref, ws_ref, bs_ref, g_go_ref,
                 qT_ref, cT_ref, k_ref, cb_ref, vT_ref, yg_ref,
                 carry_ref, sgu_ref, *, blocks_per_seq):
    @pl.when(pl.program_id(0) % blocks_per_seq == 0)
    def _():
        carry_ref[...] = jnp.zeros_like(carry_ref)

    h = (_rms(x_ref[...]) * g_mix_ref[...]).astype(BF16)

    zT = _dot_nt(w_nt_ref[...], h)
    for hd in range(FOX_HEADS):
        blk = zT[hd * FOX_DH:(hd + 1) * FOX_DH, :]
        inv = lax.rsqrt(jnp.mean(blk * blk, axis=0, keepdims=True) + EPS)
        qT_ref[0, hd * FOX_DH:(hd + 1) * FOX_DH, :] = (blk * inv * gq_ref[...]).astype(BF16)
    ones_rows = jnp.where(lax.broadcasted_iota(jnp.int32, (V_ROWS - FOX_DH, TK), 0) == 0, 1.0, 0.0).astype(BF16)
    for c in range(TM // TK):
        for hd in range(FOX_HEADS):
            vT_ref[0, c, hd * V_ROWS:hd * V_ROWS + FOX_DH, :] = (
                zT[FOX_W + hd * FOX_DH:FOX_W + (hd + 1) * FOX_DH, c * TK:(c + 1) * TK].astype(BF16))
            vT_ref[0, c, hd * V_ROWS + FOX_DH:(hd + 1) * V_ROWS, :] = ones_rows

    f = zT[2 * FOX_W:2 * FOX_W + FOX_HEADS, :] + bf_ref[...]
    logf = (jnp.minimum(f, 0.0) - jnp.log1p(jnp.exp(-jnp.abs(f)))) * LOG2E
    zeros8 = jnp.zeros((8, TM), F32)
    parts = jnp.concatenate(_split3(logf) + (zeros8,), axis=0).astype(BF16)
    cs = _dot(parts, triu_ref[...])
    c = cs[0:8] + cs[8:16] + cs[16:24] + carry_ref[:, 0:1]
    carry_ref[...] = jnp.broadcast_to(c[:, TM - 1:TM], carry_ref.shape)
    cT_ref[0] = c

    one_row = jnp.where(lax.broadcasted_iota(jnp.int32, (8, TM), 0) == 0, 1.0, 0.0)
    cparts = jnp.concatenate(_split3(c) + (one_row,), axis=0).astype(BF16)
    cb_ref[...] = _dot(place_ref[...], cparts).T.astype(BF16)

    yg_ref[...] = _gmlp_branch(h, w_nn_ref, g_sgu_ref, ws_ref, bs_ref, g_go_ref, sgu_ref)

    k = _dot(h, w_nn_ref[:, :FOX_W])
    ksum = _dot((k * k).astype(BF16), headsum_ref[...])
    k_ref[...] = (k * lax.rsqrt(ksum * (1.0 / FOX_DH) + EPS) * gk_ref[...]).astype(BF16)


def _gmlp_branch(h, w_nn_ref, g_sgu_ref, ws_ref, bs_ref, g_go_ref, sgu_ref):
    vg = _gelu_tanh(_dot(h, w_nn_ref[:, FOX_W + GMLP_W:]))
    vgn = (_rms(vg) * g_sgu_ref[...]).astype(BF16)
    u = _gelu_tanh(_dot(h, w_nn_ref[:, FOX_W:FOX_W + GMLP_W]))
    lane = lax.broadcasted_iota(jnp.int32, (CHUNK, LANES), 1)
    tril = (lax.broadcasted_iota(jnp.int32, (CHUNK, CHUNK), 0)
            >= lax.broadcasted_iota(jnp.int32, (CHUNK, CHUNK), 1))
    n_chunks = TM // CHUNK
    for p in range(GMLP_GROUPS // 2):
        lo_blocks, hi_blocks = [], []
        for ch in range(n_chunks):
            vp = vgn[ch * CHUNK:(ch + 1) * CHUNK, p * LANES:(p + 1) * LANES]
            lo_blocks.append(jnp.where(lane < GMLP_DG, vp, jnp.zeros_like(vp)))
            hi_blocks.append(jnp.where(lane >= GMLP_DG, vp, jnp.zeros_like(vp)))
        vcat = jnp.concatenate([jnp.concatenate(lo_blocks, axis=1),
                                jnp.concatenate(hi_blocks, axis=1)], axis=0)
        w0 = ws_ref[2 * p]
        w1 = ws_ref[2 * p + 1]
        wpair = jnp.concatenate([jnp.where(tril, w0, jnp.zeros_like(w0)),
                                 jnp.where(tril, w1, jnp.zeros_like(w1))], axis=1)
        mixed = _dot(wpair, vcat)
        bias = bs_ref[:, p * LANES:(p + 1) * LANES]
        for ch in range(n_chunks):
            sgu_ref[ch * CHUNK:(ch + 1) * CHUNK, p * LANES:(p + 1) * LANES] = (
                u[ch * CHUNK:(ch + 1) * CHUNK, p * LANES:(p + 1) * LANES]
                * (mixed[:, ch * LANES:(ch + 1) * LANES] + bias))
    return (_rms(sgu_ref[...]) * g_go_ref[...]).astype(BF16)


def _proj(x, batch, seq, g_mix, w_nn, w_nt, gq, gk, bfc, headsum, triu, place, g_sgu, ws, bs, g_go):
    t, d = x.shape
    bps = seq // TM
    n_nt = w_nt.shape[0]

    def rows(width):
        return pl.BlockSpec((TM, width), lambda r: (r, 0))

    kern = functools.partial(_proj_kernel, blocks_per_seq=bps)
    return pl.pallas_call(
        kern,
        grid=(t // TM,),
        in_specs=[rows(d), _resident((1, d)), _resident((d, w_nn.shape[1])), _resident((n_nt, d)),
                  _resident((FOX_DH, TM)), _resident((1, FOX_W)), _resident((FOX_HEADS, TM)),
                  _resident((FOX_W, FOX_W)), _resident((TM, TM)), _resident((PAIRS * BIAS_ROWS, 32)),
                  _resident((1, GMLP_W)), _resident((GMLP_GROUPS, CHUNK, CHUNK)),
                  _resident((CHUNK, GMLP_W)), _resident((1, GMLP_W))],
        out_specs=[pl.BlockSpec((1, FOX_W, TM), lambda r: (r // bps, 0, r % bps)),
                   pl.BlockSpec((1, FOX_HEADS, TM), lambda r: (r // bps, 0, r % bps)),
                   rows(FOX_W), rows(LANES),
                   pl.BlockSpec((1, TM // TK, FOX_HEADS * V_ROWS, TK), lambda r: (r // bps, r % bps, 0, 0)),
                   rows(GMLP_W)],
        out_shape=[jax.ShapeDtypeStruct((batch, FOX_W, seq), BF16),
                   jax.ShapeDtypeStruct((batch, FOX_HEADS, seq), F32),
                   jax.ShapeDtypeStruct((t, FOX_W), BF16),
                   jax.ShapeDtypeStruct((t, LANES), BF16),
                   jax.ShapeDtypeStruct((batch, seq // TK, FOX_HEADS * V_ROWS, TK), BF16),
                   jax.ShapeDtypeStruct((t, GMLP_W), BF16)],
        scratch_shapes=[pltpu.VMEM((FOX_HEADS, LANES), F32), pltpu.VMEM((TM, GMLP_W), F32)],
        compiler_params=_params(1),
        name="proj",
    )(x, g_mix, w_nn, w_nt, gq, gk, bfc, headsum, triu, place, g_sgu, ws, bs, g_go)


def _fox_kernel(qT_ref, cT_ref, k_ref, cb_ref, vT_ref, o_ref, qq_ref, s_ref, acc_ref):
    i = pl.program_id(1)

    row = lax.broadcasted_iota(jnp.int32, (LANES, TQ), 0)
    for p in range(PAIRS):
        qpair = qT_ref[0, p * LANES:(p + 1) * LANES, :]
        for e in range(2):
            hd = 2 * p + e
            own = (row >= FOX_DH * e) & (row < FOX_DH * (e + 1))
            qpart = jnp.where(own, qpair, jnp.zeros_like(qpair))
            hi, mid, lo = _split3(cT_ref[0, hd:hd + 1, :])
            base = BIAS_ROWS * p
            sel = (row >= base + 3 * e) & (row < base + 3 * e + 3)
            bias = jnp.where(sel, 1.0, 0.0)
            bias = jnp.where(row == base + 6, hi, bias)
            bias = jnp.where(row == base + 7, mid, bias)
            bias = jnp.where(row == base + 8, lo, bias)
            qq_ref[p, :, e * TQ:(e + 1) * TQ] = jnp.concatenate([qpart, bias.astype(BF16)], axis=0)

    acc_ref[...] = jnp.zeros_like(acc_ref)

    def scores(p, j, diagonal):
        k0 = pl.multiple_of(j * TK, TK)
        kk = jnp.concatenate([k_ref[0, pl.ds(k0, TK), p * LANES:(p + 1) * LANES],
                              cb_ref[0, pl.ds(k0, TK), :]], axis=1)
        s = _dot(kk, qq_ref[p])
        if diagonal:
            future = (lax.broadcasted_iota(jnp.int32, (TK, 2 * TQ), 0)
                      > lax.broadcasted_iota(jnp.int32, (TK, 2 * TQ), 1) % TQ)
            s = jnp.where(future, MASKED, s)
        s_ref[p] = s
        return jnp.max(s, axis=0, keepdims=True)

    def absorb(p, j, smax, m_old):
        m = jnp.maximum(m_old, smax)
        alpha = jnp.exp2(m_old - m)
        pb = jnp.exp2(s_ref[p] - m).astype(BF16)
        for e in range(2):
            hd = 2 * p + e
            pv = _dot(vT_ref[0, j, hd * V_ROWS:(hd + 1) * V_ROWS, :], pb[:, e * TQ:(e + 1) * TQ])
            acc_ref[hd] = alpha[:, e * TQ:(e + 1) * TQ] * acc_ref[hd] + pv
        return m

    def chunk(j, nxt, carry, diagonal):
        smax, ms = carry
        new_m = []
        for p in range(PAIRS):
            nxt_smax = scores(p + 1, j, diagonal) if p + 1 < PAIRS else scores(0, nxt, False)
            new_m.append(absorb(p, j, smax, ms[p]))
            smax = nxt_smax
        return smax, tuple(new_m)

    init = (scores(0, i, True), tuple(jnp.full((1, 2 * TQ), MASKED, F32) for _ in range(PAIRS)))
    carry = chunk(i, 0, init, True)
    lax.fori_loop(0, i, lambda j, cr: chunk(j, j + 1, cr, False), carry)

    outs = []
    for hd in range(FOX_HEADS):
        acc = acc_ref[hd]
        outs.append(acc[:FOX_DH] * (1.0 / acc[FOX_DH:FOX_DH + 1]))
    o_ref[0] = jnp.concatenate(outs, axis=0).T


def _fox(qT, cT, k, cb, vT):
    batch, _, seq = qT.shape
    return pl.pallas_call(
        _fox_kernel,
        grid=(batch, seq // TQ),
        in_specs=[pl.BlockSpec((1, FOX_W, TQ), lambda b, i: (b, 0, i)),
                  pl.BlockSpec((1, FOX_HEADS, TQ), lambda b, i: (b, 0, i)),
                  pl.BlockSpec((1, seq, FOX_W), lambda b, i: (b, 0, 0)),
                  pl.BlockSpec((1, seq, LANES), lambda b, i: (b, 0, 0)),
                  pl.BlockSpec((1, seq // TK, FOX_HEADS * V_ROWS, TK), lambda b, i: (b, 0, 0, 0))],
        out_specs=pl.BlockSpec((1, TQ, FOX_W), lambda b, i: (b, i, 0)),
        out_shape=jax.ShapeDtypeStruct((batch, seq, FOX_W), F32),
        scratch_shapes=[pltpu.VMEM((PAIRS, 2 * LANES, 2 * TQ), BF16),
                        pltpu.VMEM((PAIRS, TK, 2 * TQ), F32),
                        pltpu.VMEM((FOX_HEADS, V_ROWS, TQ), F32)],
        compiler_params=_params(2),
        name="fox",
    )(qT, cT, k, cb, vT)


def _memkv_kernel(mem_ref, g_mem_ref, w_ckT_ref, w_cv_ref, g_ck_ref, kcT_ref, vc_ref):
    mn = (_rms(mem_ref[0]) * g_mem_ref[...]).astype(BF16)
    kT = _dot_nt(w_ckT_ref[...], mn)
    dh = kT.shape[0] // CA_HEADS
    for hd in range(CA_HEADS):
        blk = kT[hd * dh:(hd + 1) * dh, :]
        inv = lax.rsqrt(jnp.mean(blk * blk, axis=0, keepdims=True) + EPS)
        kcT_ref[0, hd * dh:(hd + 1) * dh, :] = (blk * inv * g_ck_ref[...]).astype(BF16)
    vc_ref[0] = _dot(mn, w_cv_ref[...]).astype(BF16)


def _memkv(mem, g_mem, w_ckT, w_cv, g_ck_col):
    batch, m, d = mem.shape
    return pl.pallas_call(
        _memkv_kernel,
        grid=(batch,),
        in_specs=[pl.BlockSpec((1, m, d), lambda b: (b, 0, 0)), _resident((1, d)), _resident((d, d)),
                  _resident((d, d)), _resident((d // CA_HEADS, m))],
        out_specs=[pl.BlockSpec((1, d, m), lambda b: (b, 0, 0)), pl.BlockSpec((1, m, d), lambda b: (b, 0, 0))],
        out_shape=[jax.ShapeDtypeStruct((batch, d, m), BF16), jax.ShapeDtypeStruct((batch, m, d), BF16)],
        compiler_params=_params(1),
        name="memkv",
    )(mem, g_mem, w_ckT, w_cv, g_ck_col)


def _tail_kernel(x_ref, attn_ref, yg_ref, g_fo_ref, w_out_ref, g_ca_ref, w_cq_ref, g_cq_ref, kcT_ref, vc_ref,
                 w_co_ref, g_f2_ref, wgu_ref, wo_ref, o_ref, act_ref):
    a = (_rms(attn_ref[...]) * g_fo_ref[...]).astype(BF16)
    x2 = x_ref[...] + _dot(jnp.concatenate([a, yg_ref[...]], axis=1), w_out_ref[...])

    hq = (_rms(x2) * g_ca_ref[...]).astype(BF16)
    qc = _dot(hq, w_cq_ref[...])
    dh = qc.shape[1] // CA_HEADS
    outs = []
    for hd in range(CA_HEADS):
        qh = (_rms(qc[:, hd * dh:(hd + 1) * dh]) * g_cq_ref[...]).astype(BF16)
        s = _dot(qh, kcT_ref[0, hd * dh:(hd + 1) * dh, :])
        pexp = jnp.exp2(s - jnp.max(s, axis=-1, keepdims=True))
        denom = jnp.sum(pexp, axis=-1, keepdims=True)
        o = _dot(pexp.astype(BF16), vc_ref[0, :, hd * dh:(hd + 1) * dh]) * (1.0 / denom)
        outs.append(o.astype(BF16))
    x3 = x2 + _dot(jnp.concatenate(outs, axis=1), w_co_ref[...])

    o_ref[...] = _swiglu_half_step(x3, g_f2_ref, wgu_ref, wo_ref, act_ref)


def _tail(x1, attn, yg, seq, g_fo, w_out, g_ca, w_cq, g_cq, kcT, vc, w_co, g_f2, wgu, wo):
    t, d = x1.shape
    d_ff = wo.shape[0]
    m = vc.shape[1]
    bps = seq // TM

    def rows(width):
        return pl.BlockSpec((TM, width), lambda r: (r, 0))

    return pl.pallas_call(
        _tail_kernel,
        grid=(t // TM,),
        in_specs=[rows(d), rows(FOX_W), rows(GMLP_W), _resident((1, FOX_W)), _resident((d, d)),
                  _resident((1, d)), _resident((d, d)), _resident((1, d // CA_HEADS)),
                  pl.BlockSpec((1, d, m), lambda r: (r // bps, 0, 0)),
                  pl.BlockSpec((1, m, d), lambda r: (r // bps, 0, 0)),
                  _resident((d, d)), _resident((1, d)), _resident((d, 2 * d_ff)), _resident((d_ff, d))],
        out_specs=rows(d),
        out_shape=jax.ShapeDtypeStruct((t, d), F32),
        scratch_shapes=[pltpu.VMEM((TM, d_ff), BF16)],
        compiler_params=_params(1),
        name="tail",
    )(x1, attn, yg, g_fo, w_out, g_ca, w_cq, g_cq, kcT, vc, w_co, g_f2, wgu, wo)


def _placement():
    pm = np.zeros((PAIRS * BIAS_ROWS, 32), np.float32)
    for p in range(PAIRS):
        for e in range(2):
            for part in range(3):
                pm[BIAS_ROWS * p + 3 * e + part, 8 * part + 2 * p + e] = -1.0
        pm[BIAS_ROWS * p + 6:BIAS_ROWS * p + 9, 24] = 1.0
    return pm


def _layer(x, mem, g_ffn1, w_ffn1_in, w_ffn1_out, g_mix, w_in, b_f, g_q, g_k, g_sgu, w_s, b_s, g_fox_o,
           g_gmlp_o, w_out, g_ca, g_mem, w_cq, w_ckv, g_cq, g_ck, w_co, g_ffn2, w_ffn2_in, w_ffn2_out):
    batch, seq, d = x.shape
    t = batch * seq
    m = mem.shape[1]
    dh_ca = d // CA_HEADS

    def row(v):
        return v.reshape(1, -1).astype(F32)

    k_off, v_off, f_off, uv_off = FOX_W, 2 * FOX_W, 3 * FOX_W, 3 * FOX_W + FOX_HEADS
    w_nn = jnp.concatenate([w_in[:, k_off:v_off], w_in[:, uv_off:]], axis=1).astype(BF16)
    w_nt = jnp.concatenate([w_in[:, :k_off].T, w_in[:, v_off:f_off].T, w_in[:, f_off:uv_off].T,
                            jnp.zeros((8, d), F32)], axis=0).astype(BF16)
    gq = jnp.broadcast_to((g_q * (FOX_DH ** -0.5 * LOG2E))[:, None], (FOX_DH, TM)).astype(F32)
    gk = row(jnp.tile(g_k, FOX_HEADS))
    bfc = jnp.broadcast_to(b_f[:, None], (FOX_HEADS, TM)).astype(F32)
    headsum = jnp.asarray(np.kron(np.eye(FOX_HEADS, dtype=np.float32),
                                  np.ones((FOX_DH, FOX_DH), np.float32)), BF16)
    triu = jnp.asarray(np.triu(np.ones((TM, TM), np.float32)), BF16)
    place = jnp.asarray(_placement(), BF16)
    bs = jnp.repeat(b_s.T, GMLP_DG, axis=1).astype(F32)
    g_cq_s = row(g_cq * (dh_ca ** -0.5 * LOG2E))
    g_ck_col = jnp.broadcast_to(g_ck[:, None], (dh_ca, m)).astype(F32)

    kcT, vc = _memkv(mem, row(g_mem), w_ckv[:, :d].T.astype(BF16), w_ckv[:, d:].astype(BF16), g_ck_col)

    x1 = _ffn(x.reshape(t, d), row(g_ffn1), w_ffn1_in, w_ffn1_out)
    qT, cT, k, cb, vT, yg = _proj(x1, batch, seq, row(g_mix), w_nn, w_nt, gq, gk, bfc, headsum, triu, place,
                                  row(g_sgu), w_s.astype(BF16), bs, row(g_gmlp_o))
    attn = _fox(qT, cT, k.reshape(batch, seq, FOX_W), cb.reshape(batch, seq, LANES), vT)
    out = _tail(x1, attn.reshape(t, FOX_W), yg, seq, row(g_fox_o), w_out.astype(BF16), row(g_ca),
                w_cq.astype(BF16), g_cq_s, kcT, vc, w_co.astype(BF16), row(g_ffn2),
                w_ffn2_in.astype(BF16), w_ffn2_out.astype(BF16))
    return out.reshape(batch, seq, d)


def kernel(x, mem, g_ffn1, w_ffn1_in, w_ffn1_out, g_mix, w_in, b_f, g_q, g_k, g_sgu, w_s, b_s, g_fox_o,
           g_gmlp_o, w_out, g_ca, g_mem, w_cq, w_ckv, g_cq, g_ck, w_co, g_ffn2, w_ffn2_in, w_ffn2_out):
    layer_params = (g_ffn1, w_ffn1_in, w_ffn1_out, g_mix, w_in, b_f, g_q, g_k, g_sgu, w_s, b_s, g_fox_o,
                    g_gmlp_o, w_out, g_ca, g_mem, w_cq, w_ckv, g_cq, g_ck, w_co, g_ffn2, w_ffn2_in, w_ffn2_out)
    for l in range(g_ffn1.shape[0]):
        x = _layer(x, mem, *(p[l] for p in layer_params))
    return x
```

```python
import functools

import jax
import jax.numpy as jnp
import numpy as np
from jax import lax
from jax.experimental import pallas as pl
from jax.experimental.pallas import tpu as pltpu

F32 = jnp.float32
BF16 = jnp.bfloat16
EPS = 1e-6
LOG2E = 1.4426950408889634
MASKED = -1e30

FOX_HEADS = 8
FOX_DH = 64
FOX_W = FOX_HEADS * FOX_DH
GMLP_GROUPS = 8
GMLP_DG = 64
GMLP_W = GMLP_GROUPS * GMLP_DG
CHUNK = 128
CA_HEADS = 4

LANES = 128
VMEM_LIMIT_BYTES = 56 * 1024 * 1024

TM = 512
TQ = 512
TK = 512
FC = 256
PAIRS = FOX_HEADS // 2
BIAS_ROWS = 32
V_ROWS = FOX_DH + 16


def _rms(x):
    return x * lax.rsqrt(jnp.mean(x * x, axis=-1, keepdims=True) + EPS)


def _dot(a, b):
    return lax.dot_general(a, b, (((1,), (0,)), ((), ())), preferred_element_type=F32)


def _dot_nt(a, b):
    return lax.dot_general(a, b, (((1,), (1,)), ((), ())), preferred_element_type=F32)


def _split3(x):
    hi = x.astype(BF16).astype(F32)
    mid = (x - hi).astype(BF16).astype(F32)
    lo = (x - hi - mid).astype(BF16).astype(F32)
    return hi, mid, lo


def _resident(shape):
    return pl.BlockSpec(shape, lambda *_: (0,) * len(shape), pipeline_mode=pl.Buffered(1))


def _params(n_axes):
    return pltpu.CompilerParams(dimension_semantics=("arbitrary",) * n_axes,
                                vmem_limit_bytes=VMEM_LIMIT_BYTES)


def _swiglu_half_step(x, g_ref, wgu_ref, wo_ref, act_ref):
    d_ff = wo_ref.shape[0]
    h = (_rms(x) * g_ref[...]).astype(BF16)
    for c in range(d_ff // FC):
        gate = _dot(h, wgu_ref[:, c * FC:(c + 1) * FC])
        up = _dot(h, wgu_ref[:, d_ff + c * FC:d_ff + (c + 1) * FC])
        act_ref[:, c * FC:(c + 1) * FC] = (gate * jax.nn.sigmoid(gate) * up).astype(BF16)
    return x + 0.5 * _dot(act_ref[...], wo_ref[...])


def _ffn_kernel(x_ref, g_ref, wgu_ref, wo_ref, *refs, n_cast):
    o_ref, act_ref = refs[n_cast], refs[-1]
    for src_ref, dst_ref in zip(refs[:n_cast], refs[n_cast + 1:-1]):
        dst_ref[...] = src_ref[...].astype(BF16)
    o_ref[...] = _swiglu_half_step(x_ref[...], g_ref, wgu_ref, wo_ref, act_ref)


def _ffn(x, g, wgu, wo, to_bf16=()):
    t, d = x.shape
    d_ff = wo.shape[0]
    steps = t // TM
    row = pl.BlockSpec((TM, d), lambda r: (r, 0))
    slices = [pl.BlockSpec((w.shape[0] // steps, w.shape[1]), lambda r: (r, 0)) for w in to_bf16]
    outs = pl.pallas_call(
        functools.partial(_ffn_kernel, n_cast=len(to_bf16)),
        grid=(steps,),
        in_specs=[row, _resident((1, d)), _resident((d, 2 * d_ff)), _resident((d_ff, d))] + slices,
        out_specs=[row] + slices,
        out_shape=[jax.ShapeDtypeStruct((t, d), F32)] + [jax.ShapeDtypeStruct(w.shape, BF16) for w in to_bf16],
        scratch_shapes=[pltpu.VMEM((TM, d_ff), BF16)],
        compiler_params=_params(1),
        name="ffn",
    )(x, g, wgu, wo, *to_bf16)
    return outs[0], outs[1:]


def _gelu_tanh(x):
    c = float(np.sqrt(2.0 / np.pi))
    return x * (0.5 * jnp.tanh(x * (c + (c * 0.044715) * (x * x))) + 0.5)


def _proj_kernel(x_ref, g_mix_ref, w_nn_ref, w_nt_ref, gq_ref, gk_ref, bf_ref, headsum_ref, triu_ref,
                 place_ref, g_sgu_ref, ws_ref, bs_ref, g_go_ref,
                 qT_ref, cT_ref, k_ref, cb_ref, vT_ref, yg_ref,
                 carry_ref, sgu_ref, *, blocks_per_seq):
    @pl.when(pl.program_id(0) % blocks_per_seq == 0)
    def _():
        carry_ref[...] = jnp.zeros_like(carry_ref)

    h = (_rms(x_ref[...]) * g_mix_ref[...]).astype(BF16)

    zT = _dot_nt(w_nt_ref[...], h)
    for hd in range(FOX_HEADS):
        blk = zT[hd * FOX_DH:(hd + 1) * FOX_DH, :]
        inv = lax.rsqrt(jnp.mean(blk * blk, axis=0, keepdims=True) + EPS)
        qT_ref[0, hd * FOX_DH:(hd + 1) * FOX_DH, :] = (blk * inv * gq_ref[...]).astype(BF16)
    ones_rows = jnp.where(lax.broadcasted_iota(jnp.int32, (V_ROWS - FOX_DH, TK), 0) == 0, 1.0, 0.0).astype(BF16)
    for c in range(TM // TK):
        for hd in range(FOX_HEADS):
            vT_ref[0, c, hd * V_ROWS:hd * V_ROWS + FOX_DH, :] = (
                zT[FOX_W + hd * FOX_DH:FOX_W + (hd + 1) * FOX_DH, c * TK:(c + 1) * TK].astype(BF16))
            vT_ref[0, c, hd * V_ROWS + FOX_DH:(hd + 1) * V_ROWS, :] = ones_rows

    f = zT[2 * FOX_W:2 * FOX_W + FOX_HEADS, :] + bf_ref[...]
    logf = (jnp.minimum(f, 0.0) - jnp.log1p(jnp.exp(-jnp.abs(f)))) * LOG2E
    zeros8 = jnp.zeros((8, TM), F32)
    parts = jnp.concatenate(_split3(logf) + (zeros8,), axis=0).astype(BF16)
    cs = _dot(parts, triu_ref[...])
    c = cs[0:8] + cs[8:16] + cs[16:24] + carry_ref[:, 0:1]
    carry_ref[...] = jnp.broadcast_to(c[:, TM - 1:TM], carry_ref.shape)
    cT_ref[0] = c

    one_row = jnp.where(lax.broadcasted_iota(jnp.int32, (8, TM), 0) == 0, 1.0, 0.0)
    cparts = jnp.concatenate(_split3(c) + (one_row,), axis=0).astype(BF16)
    cb_ref[...] = _dot(place_ref[...], cparts).T.astype(BF16)

    yg_ref[...] = _gmlp_branch(h, w_nn_ref, g_sgu_ref, ws_ref, bs_ref, g_go_ref, sgu_ref)

    k = _dot(h, w_nn_ref[:, :FOX_W])
    ksum = _dot((k * k).astype(BF16), headsum_ref[...])
    k_ref[...] = (k * lax.rsqrt(ksum * (1.0 / FOX_DH) + EPS) * gk_ref[...]).astype(BF16)


def _gmlp_branch(h, w_nn_ref, g_sgu_ref, ws_ref, bs_ref, g_go_ref, sgu_ref):
    vg = _gelu_tanh(_dot(h, w_nn_ref[:, FOX_W + GMLP_W:]))
    vgn = (_rms(vg) * g_sgu_ref[...]).astype(BF16)
    u = _gelu_tanh(_dot(h, w_nn_ref[:, FOX_W:FOX_W + GMLP_W]))
    lane = lax.broadcasted_iota(jnp.int32, (CHUNK, LANES), 1)
    tril = (lax.broadcasted_iota(jnp.int32, (CHUNK, CHUNK), 0)
            >= lax.broadcasted_iota(jnp.int32, (CHUNK, CHUNK), 1))
    n_chunks = TM // CHUNK
    for p in range(GMLP_GROUPS // 2):
        lo_blocks, hi_blocks = [], []
        for ch in range(n_chunks):
            vp = vgn[ch * CHUNK:(ch + 1) * CHUNK, p * LANES:(p + 1) * LANES]
            lo_blocks.append(jnp.where(lane < GMLP_DG, vp, jnp.zeros_like(vp)))
            hi_blocks.append(jnp.where(lane >= GMLP_DG, vp, jnp.zeros_like(vp)))
        vcat = jnp.concatenate([jnp.concatenate(lo_blocks, axis=1),
                                jnp.concatenate(hi_blocks, axis=1)], axis=0)
        w0 = ws_ref[2 * p]
        w1 = ws_ref[2 * p + 1]
        wpair = jnp.concatenate([jnp.where(tril, w0, jnp.zeros_like(w0)),
                                 jnp.where(tril, w1, jnp.zeros_like(w1))], axis=1)
        mixed = _dot(wpair, vcat)
        bias = bs_ref[:, p * LANES:(p + 1) * LANES]
        for ch in range(n_chunks):
            sgu_ref[ch * CHUNK:(ch + 1) * CHUNK, p * LANES:(p + 1) * LANES] = (
                u[ch * CHUNK:(ch + 1) * CHUNK, p * LANES:(p + 1) * LANES]
                * (mixed[:, ch * LANES:(ch + 1) * LANES] + bias))
    return (_rms(sgu_ref[...]) * g_go_ref[...]).astype(BF16)


def _proj(x, batch, seq, g_mix, w_nn, w_nt, gq, gk, bfc, headsum, triu, place, g_sgu, ws, bs, g_go):
    t, d = x.shape
    bps = seq // TM
    n_nt = w_nt.shape[0]

    def rows(width):
        return pl.BlockSpec((TM, width), lambda r: (r, 0))

    kern = functools.partial(_proj_kernel, blocks_per_seq=bps)
    return pl.pallas_call(
        kern,
        grid=(t // TM,),
        in_specs=[rows(d), _resident((1, d)), _resident((d, w_nn.shape[1])), _resident((n_nt, d)),
                  _resident((FOX_DH, TM)), _resident((1, FOX_W)), _resident((FOX_HEADS, TM)),
                  _resident((FOX_W, FOX_W)), _resident((TM, TM)), _resident((PAIRS * BIAS_ROWS, 32)),
                  _resident((1, GMLP_W)), _resident((GMLP_GROUPS, CHUNK, CHUNK)),
                  _resident((CHUNK, GMLP_W)), _resident((1, GMLP_W))],
        out_specs=[pl.BlockSpec((1, FOX_W, TM), lambda r: (r // bps, 0, r % bps)),
                   pl.BlockSpec((1, FOX_HEADS, TM), lambda r: (r // bps, 0, r % bps)),
                   rows(FOX_W), rows(LANES),
                   pl.BlockSpec((1, TM // TK, FOX_HEADS * V_ROWS, TK), lambda r: (r // bps, r % bps, 0, 0)),
                   rows(GMLP_W)],
        out_shape=[jax.ShapeDtypeStruct((batch, FOX_W, seq), BF16),
                   jax.ShapeDtypeStruct((batch, FOX_HEADS, seq), F32),
                   jax.ShapeDtypeStruct((t, FOX_W), BF16),
                   jax.ShapeDtypeStruct((t, LANES), BF16),
                   jax.ShapeDtypeStruct((batch, seq // TK, FOX_HEADS * V_ROWS, TK), BF16),
                   jax.ShapeDtypeStruct((t, GMLP_W), BF16)],
        scratch_shapes=[pltpu.VMEM((FOX_HEADS, LANES), F32), pltpu.VMEM((TM, GMLP_W), F32)],
        compiler_params=_params(1),
        name="proj",
    )(x, g_mix, w_nn, w_nt, gq, gk, bfc, headsum, triu, place, g_sgu, ws, bs, g_go)


def _fox_kernel(qT_ref, cT_ref, k_ref, cb_ref, vT_ref, o_ref, qq_ref, s_ref, acc_ref):
    i = pl.program_id(1)

    row = lax.broadcasted_iota(jnp.int32, (BIAS_ROWS, TQ), 0)
    zeros_q = jnp.zeros((FOX_DH, TQ), BF16)
    for p in range(PAIRS):
        for e in range(2):
            hd = 2 * p + e
            qh = qT_ref[0, hd * FOX_DH:(hd + 1) * FOX_DH, :]
            hi, mid, lo = _split3(cT_ref[0, hd:hd + 1, :])
            blk = jnp.where((row >= 3 * e) & (row < 3 * e + 3), 1.0, 0.0)
            blk = jnp.where(row == 6, hi, blk)
            blk = jnp.where(row == 7, mid, blk)
            blk = jnp.where(row == 8, lo, blk)
            pieces = [qh, zeros_q] if e == 0 else [zeros_q, qh]
            if p > 0:
                pieces.append(jnp.zeros((BIAS_ROWS * p, TQ), BF16))
            pieces.append(blk.astype(BF16))
            if p + 1 < PAIRS:
                pieces.append(jnp.zeros((BIAS_ROWS * (PAIRS - 1 - p), TQ), BF16))
            qq_ref[p, :, e * TQ:(e + 1) * TQ] = jnp.concatenate(pieces, axis=0)

    def scores(p, j, diagonal):
        k0 = pl.multiple_of(j * TK, TK)
        kk = jnp.concatenate([k_ref[0, pl.ds(k0, TK), p * LANES:(p + 1) * LANES],
                              cb_ref[0, pl.ds(k0, TK), :]], axis=1)
        s = _dot(kk, qq_ref[p])
        if diagonal:
            future = (lax.broadcasted_iota(jnp.int32, (TK, 2 * TQ), 0)
                      > lax.broadcasted_iota(jnp.int32, (TK, 2 * TQ), 1) % TQ)
            s = jnp.where(future, MASKED, s)
        s_ref[p] = s
        return jnp.max(s, axis=0, keepdims=True)

    def absorb(p, j, smax, m_old):
        m = smax if m_old is None else jnp.maximum(m_old, smax)
        pb = jnp.exp2(s_ref[p] - m).astype(BF16)
        for e in range(2):
            hd = 2 * p + e
            pv = _dot(vT_ref[0, j, hd * V_ROWS:(hd + 1) * V_ROWS, :], pb[:, e * TQ:(e + 1) * TQ])
            if m_old is None:
                acc_ref[hd] = pv
            else:
                acc_ref[hd] = jnp.exp2(m_old - m)[:, e * TQ:(e + 1) * TQ] * acc_ref[hd] + pv
        return m

    def chunk(j, nxt, carry, diagonal):
        smax, ms = carry
        new_m = []
        for p in range(PAIRS):
            nxt_smax = scores(p + 1, j, diagonal) if p + 1 < PAIRS else scores(0, nxt, False)
            new_m.append(absorb(p, j, smax, ms[p]))
            smax = nxt_smax
        return smax, tuple(new_m)

    carry = chunk(i, 0, (scores(0, i, True), (None,) * PAIRS), True)
    lax.fori_loop(0, i, lambda j, cr: chunk(j, j + 1, cr, False), carry)

    outs = []
    for hd in range(FOX_HEADS):
        acc = acc_ref[hd]
        outs.append(acc[:FOX_DH] * (1.0 / acc[FOX_DH:FOX_DH + 1]))
    o_ref[0] = jnp.concatenate(outs, axis=0).T


def _fox(qT, cT, k, cb, vT):
    batch, _, seq = qT.shape
    return pl.pallas_call(
        _fox_kernel,
        grid=(batch, seq // TQ),
        in_specs=[pl.BlockSpec((1, FOX_W, TQ), lambda b, i: (b, 0, i)),
                  pl.BlockSpec((1, FOX_HEADS, TQ), lambda b, i: (b, 0, i)),
                  pl.BlockSpec((1, seq, FOX_W), lambda b, i: (b, 0, 0)),
                  pl.BlockSpec((1, seq, LANES), lambda b, i: (b, 0, 0)),
                  pl.BlockSpec((1, seq // TK, FOX_HEADS * V_ROWS, TK), lambda b, i: (b, 0, 0, 0))],
        out_specs=pl.BlockSpec((1, TQ, FOX_W), lambda b, i: (b, i, 0)),
        out_shape=jax.ShapeDtypeStruct((batch, seq, FOX_W), F32),
        scratch_shapes=[pltpu.VMEM((PAIRS, 2 * LANES, 2 * TQ), BF16),
                        pltpu.VMEM((PAIRS, TK, 2 * TQ), F32),
                        pltpu.VMEM((FOX_HEADS, V_ROWS, TQ), F32)],
        compiler_params=_params(2),
        name="fox",
    )(qT, cT, k, cb, vT)


def _memkv_kernel(mem_ref, g_mem_ref, w_ckT_ref, w_cv_ref, g_ck_ref, kcT_ref, vc_ref):
    mn = (_rms(mem_ref[0]) * g_mem_ref[...]).astype(BF16)
    kT = _dot_nt(w_ckT_ref[...], mn)
    dh = kT.shape[0] // CA_HEADS
    for hd in range(CA_HEADS):
        blk = kT[hd * dh:(hd + 1) * dh, :]
        inv = lax.rsqrt(jnp.mean(blk * blk, axis=0, keepdims=True) + EPS)
        kcT_ref[0, hd * dh:(hd + 1) * dh, :] = (blk * inv * g_ck_ref[...]).astype(BF16)
    vc_ref[0] = _dot(mn, w_cv_ref[...]).astype(BF16)


def _memkv(mem, g_mem, w_ckT, w_cv, g_ck_col):
    batch, m, d = mem.shape
    return pl.pallas_call(
        _memkv_kernel,
        grid=(batch,),
        in_specs=[pl.BlockSpec((1, m, d), lambda b: (b, 0, 0)), _resident((1, d)), _resident((d, d)),
                  _resident((d, d)), _resident((d // CA_HEADS, m))],
        out_specs=[pl.BlockSpec((1, d, m), lambda b: (b, 0, 0)), pl.BlockSpec((1, m, d), lambda b: (b, 0, 0))],
        out_shape=[jax.ShapeDtypeStruct((batch, d, m), BF16), jax.ShapeDtypeStruct((batch, m, d), BF16)],
        compiler_params=_params(1),
        name="memkv",
    )(mem, g_mem, w_ckT, w_cv, g_ck_col)


def _tail_kernel(x_ref, attn_ref, yg_ref, g_fo_ref, w_out_ref, g_ca_ref, w_cq_ref, g_cq_ref, kcT_ref, vc_ref,
                 w_co_ref, g_f2_ref, wgu_ref, wo_ref, o_ref, act_ref):
    a = (_rms(attn_ref[...]) * g_fo_ref[...]).astype(BF16)
    x2 = x_ref[...] + _dot(jnp.concatenate([a, yg_ref[...]], axis=1), w_out_ref[...])

    hq = (_rms(x2) * g_ca_ref[...]).astype(BF16)
    qc = _dot(hq, w_cq_ref[...])
    dh = qc.shape[1] // CA_HEADS
    outs = []
    for hd in range(CA_HEADS):
        qh = (_rms(qc[:, hd * dh:(hd + 1) * dh]) * g_cq_ref[...]).astype(BF16)
        s = _dot(qh, kcT_ref[0, hd * dh:(hd + 1) * dh, :])
        pexp = jnp.exp2(s - jnp.max(s, axis=-1, keepdims=True))
        denom = jnp.sum(pexp, axis=-1, keepdims=True)
        o = _dot(pexp.astype(BF16), vc_ref[0, :, hd * dh:(hd + 1) * dh]) * (1.0 / denom)
        outs.append(o.astype(BF16))
    x3 = x2 + _dot(jnp.concatenate(outs, axis=1), w_co_ref[...])

    o_ref[...] = _swiglu_half_step(x3, g_f2_ref, wgu_ref, wo_ref, act_ref)


def _tail(x1, attn, yg, seq, g_fo, w_out, g_ca, w_cq, g_cq, kcT, vc, w_co, g_f2, wgu, wo):
    t, d = x1.shape
    d_ff = wo.shape[0]
    m = vc.shape[1]
    bps = seq // TM

    def rows(width):
        return pl.BlockSpec((TM, width), lambda r: (r, 0))

    return pl.pallas_call(
        _tail_kernel,
        grid=(t // TM,),
        in_specs=[rows(d), rows(FOX_W), rows(GMLP_W), _resident((1, FOX_W)), _resident((d, d)),
                  _resident((1, d)), _resident((d, d)), _resident((1, d // CA_HEADS)),
                  pl.BlockSpec((1, d, m), lambda r: (r // bps, 0, 0)),
                  pl.BlockSpec((1, m, d), lambda r: (r // bps, 0, 0)),
                  _resident((d, d)), _resident((1, d)), _resident((d, 2 * d_ff)), _resident((d_ff, d))],
        out_specs=rows(d),
        out_shape=jax.ShapeDtypeStruct((t, d), F32),
        scratch_shapes=[pltpu.VMEM((TM, d_ff), BF16)],
        compiler_params=_params(1),
        name="tail",
    )(x1, attn, yg, g_fo, w_out, g_ca, w_cq, g_cq, kcT, vc, w_co, g_f2, wgu, wo)


def _placement():
    pm = np.zeros((PAIRS * BIAS_ROWS, 32), np.float32)
    for p in range(PAIRS):
        for e in range(2):
            for part in range(3):
                pm[BIAS_ROWS * p + 3 * e + part, 8 * part + 2 * p + e] = -1.0
        pm[BIAS_ROWS * p + 6:BIAS_ROWS * p + 9, 24] = 1.0
    return pm


def _layer(x, mem, g_ffn1, w_ffn1_in, w_ffn1_out, g_mix, w_in, b_f, g_q, g_k, g_sgu, w_s, b_s, g_fox_o,
           g_gmlp_o, w_out, g_ca, g_mem, w_cq, w_ckv, g_cq, g_ck, w_co, g_ffn2, w_ffn2_in, w_ffn2_out):
    batch, seq, d = x.shape
    t = batch * seq
    m = mem.shape[1]
    dh_ca = d // CA_HEADS

    def row(v):
        return v.reshape(1, -1).astype(F32)

    k_off, v_off, f_off, uv_off = FOX_W, 2 * FOX_W, 3 * FOX_W, 3 * FOX_W + FOX_HEADS
    w_nn = jnp.concatenate([w_in[:, k_off:v_off], w_in[:, uv_off:]], axis=1).astype(BF16)
    w_nt = jnp.concatenate([w_in[:, :k_off].T, w_in[:, v_off:f_off].T, w_in[:, f_off:uv_off].T,
                            jnp.zeros((8, d), F32)], axis=0).astype(BF16)
    gq = jnp.broadcast_to((g_q * (FOX_DH ** -0.5 * LOG2E))[:, None], (FOX_DH, TM)).astype(F32)
    gk = row(jnp.tile(g_k, FOX_HEADS))
    bfc = jnp.broadcast_to(b_f[:, None], (FOX_HEADS, TM)).astype(F32)
    headsum = jnp.asarray(np.kron(np.eye(FOX_HEADS, dtype=np.float32),
                                  np.ones((FOX_DH, FOX_DH), np.float32)), BF16)
    triu = jnp.asarray(np.triu(np.ones((TM, TM), np.float32)), BF16)
    place = jnp.asarray(_placement(), BF16)
    bs = jnp.repeat(b_s.T, GMLP_DG, axis=1).astype(F32)
    g_cq_s = row(g_cq * (dh_ca ** -0.5 * LOG2E))
    g_ck_col = jnp.broadcast_to(g_ck[:, None], (dh_ca, m)).astype(F32)

    kcT, vc = _memkv(mem, row(g_mem), w_ckv[:, :d].T.astype(BF16), w_ckv[:, d:].astype(BF16), g_ck_col)

    d_ff = w_ffn2_out.shape[0]
    x1, (w2_in, w2_out, w_out_b, w_cq_b, w_co_b) = _ffn(
        x.reshape(t, d), row(g_ffn1), w_ffn1_in, w_ffn1_out,
        (w_ffn2_in, w_ffn2_out.reshape(d_ff * d // (2 * d_ff), 2 * d_ff), w_out, w_cq, w_co))
    qT, cT, k, cb, vT, yg = _proj(x1, batch, seq, row(g_mix), w_nn, w_nt, gq, gk, bfc, headsum, triu, place,
                                  row(g_sgu), w_s.astype(BF16), bs, row(g_gmlp_o))
    attn = _fox(qT, cT, k.reshape(batch, seq, FOX_W), cb.reshape(batch, seq, LANES), vT)
    out = _tail(x1, attn.reshape(t, FOX_W), yg, seq, row(g_fox_o), w_out_b, row(g_ca),
                w_cq_b, g_cq_s, kcT, vc, w_co_b, row(g_ffn2), w2_in, w2_out.reshape(d_ff, d))
    return out.reshape(batch, seq, d)


def kernel(x, mem, g_ffn1, w_ffn1_in, w_ffn1_out, g_mix, w_in, b_f, g_q, g_k, g_sgu, w_s, b_s, g_fox_o,
           g_gmlp_o, w_out, g_ca, g_mem, w_cq, w_ckv, g_cq, g_ck, w_co, g_ffn2, w_ffn2_in, w_ffn2_out):
    layer_params = (g_ffn1, w_ffn1_in, w_ffn1_out, g_mix, w_in, b_f, g_q, g_k, g_sgu, w_s, b_s, g_fox_o,
                    g_gmlp_o, w_out, g_ca, g_mem, w_cq, w_ckv, g_cq, g_ck, w_co, g_ffn2, w_ffn2_in, w_ffn2_out)
    for l in range(g_ffn1.shape[0]):
        x = _layer(x, mem, *(p[l] for p in layer_params))
    return x
```

```python
import functools

import jax
import jax.numpy as jnp
import numpy as np
from jax import lax
from jax.experimental import pallas as pl
from jax.experimental.pallas import tpu as pltpu

F32 = jnp.float32
BF16 = jnp.bfloat16
EPS = 1e-6
LOG2E = 1.4426950408889634
MASKED = -1e30

FOX_HEADS = 8
FOX_DH = 64
FOX_W = FOX_HEADS * FOX_DH
GMLP_GROUPS = 8
GMLP_DG = 64
GMLP_W = GMLP_GROUPS * GMLP_DG
CHUNK = 128
CA_HEADS = 4

LANES = 128
VMEM_LIMIT_BYTES = 56 * 1024 * 1024

TM = 512
TQ = 512
TK = 512
FC = 256
PAIRS = FOX_HEADS // 2
BIAS_ROWS = 32
V_ROWS = FOX_DH + 16


def _rms(x):
    return x * lax.rsqrt(jnp.mean(x * x, axis=-1, keepdims=True) + EPS)


def _dot(a, b):
    return lax.dot_general(a, b, (((1,), (0,)), ((), ())), preferred_element_type=F32)


def _dot_nt(a, b):
    return lax.dot_general(a, b, (((1,), (1,)), ((), ())), preferred_element_type=F32)


def _split3(x):
    hi = x.astype(BF16).astype(F32)
    mid = (x - hi).astype(BF16).astype(F32)
    lo = (x - hi - mid).astype(BF16).astype(F32)
    return hi, mid, lo


def _resident(shape):
    return pl.BlockSpec(shape, lambda *_: (0,) * len(shape), pipeline_mode=pl.Buffered(1))


def _params(n_axes):
    return pltpu.CompilerParams(dimension_semantics=("arbitrary",) * n_axes,
                                vmem_limit_bytes=VMEM_LIMIT_BYTES)


def _swiglu_half_step(x, g_ref, wgu_ref, wo_ref, act_ref):
    d_ff = wo_ref.shape[0]
    h = (_rms(x) * g_ref[...]).astype(BF16)
    for c in range(d_ff // FC):
        gate = _dot(h, wgu_ref[:, c * FC:(c + 1) * FC])
        up = _dot(h, wgu_ref[:, d_ff + c * FC:d_ff + (c + 1) * FC])
        act_ref[:, c * FC:(c + 1) * FC] = (gate * jax.nn.sigmoid(gate) * up).astype(BF16)
    return x + 0.5 * _dot(act_ref[...], wo_ref[...])


def _ffn_kernel(x_ref, g_ref, wgu_ref, wo_ref, *refs, n_cast):
    o_ref, act_ref = refs[n_cast], refs[-1]
    for src_ref, dst_ref in zip(refs[:n_cast], refs[n_cast + 1:-1]):
        dst_ref[...] = src_ref[0].astype(BF16)
    o_ref[...] = _swiglu_half_step(x_ref[...], g_ref, wgu_ref.at[0], wo_ref.at[0], act_ref)


def _cast_rows(rows, steps):
    return next(rb for rb in range(16, rows + 1, 16) if rows % rb == 0 and rows // rb <= steps)


def _ffn(x, g, wgu, wo, layer, to_bf16=()):
    t, d = x.shape
    d_ff = wo.shape[1]
    steps = t // TM
    row = pl.BlockSpec((TM, d), lambda r: (r, 0))

    def stacked(shape):
        return pl.BlockSpec((1,) + shape, lambda r: (layer, 0, 0), pipeline_mode=pl.Buffered(1))

    srcs, dsts = [], []
    for w in to_bf16:
        rb = _cast_rows(w.shape[1], steps)
        last = w.shape[1] // rb - 1
        srcs.append(pl.BlockSpec((1, rb, w.shape[2]), lambda r, last=last: (layer, jnp.minimum(r, last), 0)))
        dsts.append(pl.BlockSpec((rb, w.shape[2]), lambda r, last=last: (jnp.minimum(r, last), 0)))
    outs = pl.pallas_call(
        functools.partial(_ffn_kernel, n_cast=len(to_bf16)),
        grid=(steps,),
        in_specs=[row, _resident((1, d)), stacked((d, 2 * d_ff)), stacked((d_ff, d))] + srcs,
        out_specs=[row] + dsts,
        out_shape=[jax.ShapeDtypeStruct((t, d), F32)] + [jax.ShapeDtypeStruct(w.shape[1:], BF16) for w in to_bf16],
        scratch_shapes=[pltpu.VMEM((TM, d_ff), BF16)],
        compiler_params=_params(1),
        name="ffn",
    )(x, g, wgu, wo, *to_bf16)
    return outs[0], outs[1:]


def _gelu_tanh(x):
    c = float(np.sqrt(2.0 / np.pi))
    return x * (0.5 * jnp.tanh(x * (c + (c * 0.044715) * (x * x))) + 0.5)


def _proj_kernel(x_ref, g_mix_ref, w_nn_ref, w_nt_ref, gq_ref, gk_ref, bf_ref, headsum_ref, triu_ref,
                 place_ref, g_sgu_ref, ws_ref, bs_ref, g_go_ref,
                 qT_ref, cT_ref, k_ref, cb_ref, vT_ref, yg_ref,
                 carry_ref, sgu_ref, *, blocks_per_seq):
    @pl.when(pl.program_id(0) % blocks_per_seq == 0)
    def _():
        carry_ref[...] = jnp.zeros_like(carry_ref)

    h = (_rms(x_ref[...]) * g_mix_ref[...]).astype(BF16)

    zT = _dot_nt(w_nt_ref[...], h)
    for hd in range(FOX_HEADS):
        blk = zT[hd * FOX_DH:(hd + 1) * FOX_DH, :]
        inv = lax.rsqrt(jnp.mean(blk * blk, axis=0, keepdims=True) + EPS)
        qT_ref[0, hd * FOX_DH:(hd + 1) * FOX_DH, :] = (blk * inv * gq_ref[...]).astype(BF16)
    ones_rows = jnp.where(lax.broadcasted_iota(jnp.int32, (V_ROWS - FOX_DH, TK), 0) == 0, 1.0, 0.0).astype(BF16)
    for c in range(TM // TK):
        for hd in range(FOX_HEADS):
            vT_ref[0, c, hd * V_ROWS:hd * V_ROWS + FOX_DH, :] = (
                zT[FOX_W + hd * FOX_DH:FOX_W + (hd + 1) * FOX_DH, c * TK:(c + 1) * TK].astype(BF16))
            vT_ref[0, c, hd * V_ROWS + FOX_DH:(hd + 1) * V_ROWS, :] = ones_rows

    f = zT[2 * FOX_W:2 * FOX_W + FOX_HEADS, :] + bf_ref[...]
    logf = (jnp.minimum(f, 0.0) - jnp.log1p(jnp.exp(-jnp.abs(f)))) * LOG2E
    zeros8 = jnp.zeros((8, TM), F32)
    parts = jnp.concatenate(_split3(logf) + (zeros8,), axis=0).astype(BF16)
    cs = _dot(parts, triu_ref[...])
    c = cs[0:8] + cs[8:16] + cs[16:24] + carry_ref[:, 0:1]
    carry_ref[...] = jnp.broadcast_to(c[:, TM - 1:TM], carry_ref.shape)
    cT_ref[0] = c

    one_row = jnp.where(lax.broadcasted_iota(jnp.int32, (8, TM), 0) == 0, 1.0, 0.0)
    cparts = jnp.concatenate(_split3(c) + (one_row,), axis=0).astype(BF16)
    cb_ref[...] = _dot(place_ref[...], cparts).T.astype(BF16)

    yg_ref[...] = _gmlp_branch(h, w_nn_ref, g_sgu_ref, ws_ref, bs_ref, g_go_ref, sgu_ref)

    k = _dot(h, w_nn_ref[:, :FOX_W])
    ksum = _dot((k * k).astype(BF16), headsum_ref[...])
    k_ref[...] = (k * lax.rsqrt(ksum * (1.0 / FOX_DH) + EPS) * gk_ref[...]).astype(BF16)


def _gmlp_branch(h, w_nn_ref, g_sgu_ref, ws_ref, bs_ref, g_go_ref, sgu_ref):
    vg = _gelu_tanh(_dot(h, w_nn_ref[:, FOX_W + GMLP_W:]))
    vgn = (_rms(vg) * g_sgu_ref[...]).astype(BF16)
    u = _gelu_tanh(_dot(h, w_nn_ref[:, FOX_W:FOX_W + GMLP_W]))
    lane = lax.broadcasted_iota(jnp.int32, (CHUNK, LANES), 1)
    tril = (lax.broadcasted_iota(jnp.int32, (CHUNK, CHUNK), 0)
            >= lax.broadcasted_iota(jnp.int32, (CHUNK, CHUNK), 1))
    n_chunks = TM // CHUNK
    for p in range(GMLP_GROUPS // 2):
        lo_blocks, hi_blocks = [], []
        for ch in range(n_chunks):
            vp = vgn[ch * CHUNK:(ch + 1) * CHUNK, p * LANES:(p + 1) * LANES]
            lo_blocks.append(jnp.where(lane < GMLP_DG, vp, jnp.zeros_like(vp)))
            hi_blocks.append(jnp.where(lane >= GMLP_DG, vp, jnp.zeros_like(vp)))
        vcat = jnp.concatenate([jnp.concatenate(lo_blocks, axis=1),
                                jnp.concatenate(hi_blocks, axis=1)], axis=0)
        w0 = ws_ref[2 * p]
        w1 = ws_ref[2 * p + 1]
        wpair = jnp.concatenate([jnp.where(tril, w0, jnp.zeros_like(w0)),
                                 jnp.where(tril, w1, jnp.zeros_like(w1))], axis=1)
        mixed = _dot(wpair, vcat)
        bias = bs_ref[:, p * LANES:(p + 1) * LANES]
        for ch in range(n_chunks):
            sgu_ref[ch * CHUNK:(ch + 1) * CHUNK, p * LANES:(p + 1) * LANES] = (
                u[ch * CHUNK:(ch + 1) * CHUNK, p * LANES:(p + 1) * LANES]
                * (mixed[:, ch * LANES:(ch + 1) * LANES] + bias))
    return (_rms(sgu_ref[...]) * g_go_ref[...]).astype(BF16)


def _proj(x, batch, seq, g_mix, w_nn, w_nt, gq, gk, bfc, headsum, triu, place, g_sgu, ws, bs, g_go):
    t, d = x.shape
    bps = seq // TM
    n_nt = w_nt.shape[0]

    def rows(width):
        return pl.BlockSpec((TM, width), lambda r: (r, 0))

    kern = functools.partial(_proj_kernel, blocks_per_seq=bps)
    return pl.pallas_call(
        kern,
        grid=(t // TM,),
        in_specs=[rows(d), _resident((1, d)), _resident((d, w_nn.shape[1])), _resident((n_nt, d)),
                  _resident((FOX_DH, TM)), _resident((1, FOX_W)), _resident((FOX_HEADS, TM)),
                  _resident((FOX_W, FOX_W)), _resident((TM, TM)), _resident((PAIRS * BIAS_ROWS, 32)),
                  _resident((1, GMLP_W)), _resident((GMLP_GROUPS, CHUNK, CHUNK)),
                  _resident((CHUNK, GMLP_W)), _resident((1, GMLP_W))],
        out_specs=[pl.BlockSpec((1, FOX_W, TM), lambda r: (r // bps, 0, r % bps)),
                   pl.BlockSpec((1, FOX_HEADS, TM), lambda r: (r // bps, 0, r % bps)),
                   rows(FOX_W), rows(LANES),
                   pl.BlockSpec((1, TM // TK, FOX_HEADS * V_ROWS, TK), lambda r: (r // bps, r % bps, 0, 0)),
                   rows(GMLP_W)],
        out_shape=[jax.ShapeDtypeStruct((batch, FOX_W, seq), BF16),
                   jax.ShapeDtypeStruct((batch, FOX_HEADS, seq), F32),
                   jax.ShapeDtypeStruct((t, FOX_W), BF16),
                   jax.ShapeDtypeStruct((t, LANES), BF16),
                   jax.ShapeDtypeStruct((batch, seq // TK, FOX_HEADS * V_ROWS, TK), BF16),
                   jax.ShapeDtypeStruct((t, GMLP_W), BF16)],
        scratch_shapes=[pltpu.VMEM((FOX_HEADS, LANES), F32), pltpu.VMEM((TM, GMLP_W), F32)],
        compiler_params=_params(1),
        name="proj",
    )(x, g_mix, w_nn, w_nt, gq, gk, bfc, headsum, triu, place, g_sgu, ws, bs, g_go)


def _fox_kernel(qT_ref, cT_ref, k_ref, cb_ref, vT_ref, o_ref, qq_ref, s_ref, acc_ref):
    i = pl.program_id(1)

    row = lax.broadcasted_iota(jnp.int32, (BIAS_ROWS, TQ), 0)
    zeros_q = jnp.zeros((FOX_DH, TQ), BF16)
    for p in range(PAIRS):
        for e in range(2):
            hd = 2 * p + e
            qh = qT_ref[0, hd * FOX_DH:(hd + 1) * FOX_DH, :]
            hi, mid, lo = _split3(cT_ref[0, hd:hd + 1, :])
            blk = jnp.where((row >= 3 * e) & (row < 3 * e + 3), 1.0, 0.0)
            blk = jnp.where(row == 6, hi, blk)
            blk = jnp.where(row == 7, mid, blk)
            blk = jnp.where(row == 8, lo, blk)
            pieces = [qh, zeros_q] if e == 0 else [zeros_q, qh]
            if p > 0:
                pieces.append(jnp.zeros((BIAS_ROWS * p, TQ), BF16))
            pieces.append(blk.astype(BF16))
            if p + 1 < PAIRS:
                pieces.append(jnp.zeros((BIAS_ROWS * (PAIRS - 1 - p), TQ), BF16))
            qq_ref[p, :, e * TQ:(e + 1) * TQ] = jnp.concatenate(pieces, axis=0)

    def scores(p, j, diagonal):
        k0 = pl.multiple_of(j * TK, TK)
        kk = jnp.concatenate([k_ref[0, pl.ds(k0, TK), p * LANES:(p + 1) * LANES],
                              cb_ref[0, pl.ds(k0, TK), :]], axis=1)
        s = _dot(kk, qq_ref[p])
        if diagonal:
            future = (lax.broadcasted_iota(jnp.int32, (TK, 2 * TQ), 0)
                      > lax.broadcasted_iota(jnp.int32, (TK, 2 * TQ), 1) % TQ)
            s = jnp.where(future, MASKED, s)
        s_ref[p] = s
        return jnp.max(s, axis=0, keepdims=True)

    def absorb(p, j, smax, m_old):
        m = smax if m_old is None else jnp.maximum(m_old, smax)
        pb = jnp.exp2(s_ref[p] - m).astype(BF16)
        for e in range(2):
            hd = 2 * p + e
            pv = _dot(vT_ref[0, j, hd * V_ROWS:(hd + 1) * V_ROWS, :], pb[:, e * TQ:(e + 1) * TQ])
            if m_old is None:
                acc_ref[hd] = pv
            else:
                acc_ref[hd] = jnp.exp2(m_old - m)[:, e * TQ:(e + 1) * TQ] * acc_ref[hd] + pv
        return m

    def chunk(j, nxt, carry, diagonal):
        smax, ms = carry
        new_m = []
        for p in range(PAIRS):
            nxt_smax = scores(p + 1, j, diagonal) if p + 1 < PAIRS else scores(0, nxt, False)
            new_m.append(absorb(p, j, smax, ms[p]))
            smax = nxt_smax
        return smax, tuple(new_m)

    carry = chunk(i, 0, (scores(0, i, True), (None,) * PAIRS), True)
    lax.fori_loop(0, i, lambda j, cr: chunk(j, j + 1, cr, False), carry)

    outs = []
    for hd in range(FOX_HEADS):
        acc = acc_ref[hd]
        outs.append(acc[:FOX_DH] * (1.0 / acc[FOX_DH:FOX_DH + 1]))
    o_ref[0] = jnp.concatenate(outs, axis=0).T


def _fox(qT, cT, k, cb, vT):
    batch, _, seq = qT.shape
    return pl.pallas_call(
        _fox_kernel,
        grid=(batch, seq // TQ),
        in_specs=[pl.BlockSpec((1, FOX_W, TQ), lambda b, i: (b, 0, i)),
                  pl.BlockSpec((1, FOX_HEADS, TQ), lambda b, i: (b, 0, i)),
                  pl.BlockSpec((1, seq, FOX_W), lambda b, i: (b, 0, 0)),
                  pl.BlockSpec((1, seq, LANES), lambda b, i: (b, 0, 0)),
                  pl.BlockSpec((1, seq // TK, FOX_HEADS * V_ROWS, TK), lambda b, i: (b, 0, 0, 0))],
        out_specs=pl.BlockSpec((1, TQ, FOX_W), lambda b, i: (b, i, 0)),
        out_shape=jax.ShapeDtypeStruct((batch, seq, FOX_W), F32),
        scratch_shapes=[pltpu.VMEM((PAIRS, 2 * LANES, 2 * TQ), BF16),
                        pltpu.VMEM((PAIRS, TK, 2 * TQ), F32),
                        pltpu.VMEM((FOX_HEADS, V_ROWS, TQ), F32)],
        compiler_params=_params(2),
        name="fox",
    )(qT, cT, k, cb, vT)


def _memkv_kernel(mem_ref, g_mem_ref, w_ckT_ref, w_cv_ref, g_ck_ref, kcT_ref, vc_ref):
    mn = (_rms(mem_ref[0]) * g_mem_ref[...]).astype(BF16)
    kT = _dot_nt(w_ckT_ref[...], mn)
    dh = kT.shape[0] // CA_HEADS
    for hd in range(CA_HEADS):
        blk = kT[hd * dh:(hd + 1) * dh, :]
        inv = lax.rsqrt(jnp.mean(blk * blk, axis=0, keepdims=True) + EPS)
        kcT_ref[0, hd * dh:(hd + 1) * dh, :] = (blk * inv * g_ck_ref[...]).astype(BF16)
    vc_ref[0] = _dot(mn, w_cv_ref[...]).astype(BF16)


def _memkv(mem, g_mem, w_ckT, w_cv, g_ck_col):
    batch, m, d = mem.shape
    return pl.pallas_call(
        _memkv_kernel,
        grid=(batch,),
        in_specs=[pl.BlockSpec((1, m, d), lambda b: (b, 0, 0)), _resident((1, d)), _resident((d, d)),
                  _resident((d, d)), _resident((d // CA_HEADS, m))],
        out_specs=[pl.BlockSpec((1, d, m), lambda b: (b, 0, 0)), pl.BlockSpec((1, m, d), lambda b: (b, 0, 0))],
        out_shape=[jax.ShapeDtypeStruct((batch, d, m), BF16), jax.ShapeDtypeStruct((batch, m, d), BF16)],
        compiler_params=_params(1),
        name="memkv",
    )(mem, g_mem, w_ckT, w_cv, g_ck_col)


def _tail_kernel(x_ref, attn_ref, yg_ref, g_fo_ref, w_out_ref, g_ca_ref, w_cq_ref, g_cq_ref, kcT_ref, vc_ref,
                 w_co_ref, g_f2_ref, wgu_ref, wo_ref, o_ref, act_ref):
    a = (_rms(attn_ref[...]) * g_fo_ref[...]).astype(BF16)
    x2 = x_ref[...] + _dot(jnp.concatenate([a, yg_ref[...]], axis=1), w_out_ref[...])

    hq = (_rms(x2) * g_ca_ref[...]).astype(BF16)
    qc = _dot(hq, w_cq_ref[...])
    dh = qc.shape[1] // CA_HEADS
    outs = []
    for hd in range(CA_HEADS):
        qh = (_rms(qc[:, hd * dh:(hd + 1) * dh]) * g_cq_ref[...]).astype(BF16)
        s = _dot(qh, kcT_ref[0, hd * dh:(hd + 1) * dh, :])
        pexp = jnp.exp2(s - jnp.max(s, axis=-1, keepdims=True))
        denom = jnp.sum(pexp, axis=-1, keepdims=True)
        o = _dot(pexp.astype(BF16), vc_ref[0, :, hd * dh:(hd + 1) * dh]) * (1.0 / denom)
        outs.append(o.astype(BF16))
    x3 = x2 + _dot(jnp.concatenate(outs, axis=1), w_co_ref[...])

    o_ref[...] = _swiglu_half_step(x3, g_f2_ref, wgu_ref, wo_ref, act_ref)


def _tail(x1, attn, yg, seq, g_fo, w_out, g_ca, w_cq, g_cq, kcT, vc, w_co, g_f2, wgu, wo):
    t, d = x1.shape
    d_ff = wo.shape[0]
    m = vc.shape[1]
    bps = seq // TM

    def rows(width):
        return pl.BlockSpec((TM, width), lambda r: (r, 0))

    return pl.pallas_call(
        _tail_kernel,
        grid=(t // TM,),
        in_specs=[rows(d), rows(FOX_W), rows(GMLP_W), _resident((1, FOX_W)), _resident((d, d)),
                  _resident((1, d)), _resident((d, d)), _resident((1, d // CA_HEADS)),
                  pl.BlockSpec((1, d, m), lambda r: (r // bps, 0, 0)),
                  pl.BlockSpec((1, m, d), lambda r: (r // bps, 0, 0)),
                  _resident((d, d)), _resident((1, d)), _resident((d, 2 * d_ff)), _resident((d_ff, d))],
        out_specs=rows(d),
        out_shape=jax.ShapeDtypeStruct((t, d), F32),
        scratch_shapes=[pltpu.VMEM((TM, d_ff), BF16)],
        compiler_params=_params(1),
        name="tail",
    )(x1, attn, yg, g_fo, w_out, g_ca, w_cq, g_cq, kcT, vc, w_co, g_f2, wgu, wo)


def _placement():
    pm = np.zeros((PAIRS * BIAS_ROWS, 32), np.float32)
    for p in range(PAIRS):
        for e in range(2):
            for part in range(3):
                pm[BIAS_ROWS * p + 3 * e + part, 8 * part + 2 * p + e] = -1.0
        pm[BIAS_ROWS * p + 6:BIAS_ROWS * p + 9, 24] = 1.0
    return pm


def _layer(layer, x, mem, g_ffn1, w_ffn1_in, w_ffn1_out, g_mix, w_in, b_f, g_q, g_k, g_sgu, w_s, b_s, g_fox_o,
           g_gmlp_o, w_out, g_ca, g_mem, w_cq, w_ckv, g_cq, g_ck, w_co, g_ffn2, w_ffn2_in, w_ffn2_out):
    (g_ffn1, g_mix, w_in, b_f, g_q, g_k, g_sgu, w_s, b_s, g_fox_o, g_gmlp_o, g_ca, g_mem, w_ckv, g_cq, g_ck,
     g_ffn2) = (v[layer] for v in (g_ffn1, g_mix, w_in, b_f, g_q, g_k, g_sgu, w_s, b_s, g_fox_o, g_gmlp_o, g_ca,
                                   g_mem, w_ckv, g_cq, g_ck, g_ffn2))
    batch, seq, d = x.shape
    t = batch * seq
    m = mem.shape[1]
    dh_ca = d // CA_HEADS

    def row(v):
        return v.reshape(1, -1).astype(F32)

    k_off, v_off, f_off, uv_off = FOX_W, 2 * FOX_W, 3 * FOX_W, 3 * FOX_W + FOX_HEADS
    w_nn = jnp.concatenate([w_in[:, k_off:v_off], w_in[:, uv_off:]], axis=1).astype(BF16)
    w_nt = jnp.concatenate([w_in[:, :k_off].T, w_in[:, v_off:f_off].T, w_in[:, f_off:uv_off].T,
                            jnp.zeros((8, d), F32)], axis=0).astype(BF16)
    gq = jnp.broadcast_to((g_q * (FOX_DH ** -0.5 * LOG2E))[:, None], (FOX_DH, TM)).astype(F32)
    gk = row(jnp.tile(g_k, FOX_HEADS))
    bfc = jnp.broadcast_to(b_f[:, None], (FOX_HEADS, TM)).astype(F32)
    headsum = jnp.asarray(np.kron(np.eye(FOX_HEADS, dtype=np.float32),
                                  np.ones((FOX_DH, FOX_DH), np.float32)), BF16)
    triu = jnp.asarray(np.triu(np.ones((TM, TM), np.float32)), BF16)
    place = jnp.asarray(_placement(), BF16)
    bs = jnp.repeat(b_s.T, GMLP_DG, axis=1).astype(F32)
    g_cq_s = row(g_cq * (dh_ca ** -0.5 * LOG2E))
    g_ck_col = jnp.broadcast_to(g_ck[:, None], (dh_ca, m)).astype(F32)

    kcT, vc = _memkv(mem, row(g_mem), w_ckv[:, :d].T.astype(BF16), w_ckv[:, d:].astype(BF16), g_ck_col)

    x1, (w2_in, w2_out, w_out_b, w_cq_b, w_co_b) = _ffn(
        x.reshape(t, d), row(g_ffn1), w_ffn1_in, w_ffn1_out, layer, (w_ffn2_in, w_ffn2_out, w_out, w_cq, w_co))
    qT, cT, k, cb, vT, yg = _proj(x1, batch, seq, row(g_mix), w_nn, w_nt, gq, gk, bfc, headsum, triu, place,
                                  row(g_sgu), w_s.astype(BF16), bs, row(g_gmlp_o))
    attn = _fox(qT, cT, k.reshape(batch, seq, FOX_W), cb.reshape(batch, seq, LANES), vT)
    out = _tail(x1, attn.reshape(t, FOX_W), yg, seq, row(g_fox_o), w_out_b, row(g_ca),
                w_cq_b, g_cq_s, kcT, vc, w_co_b, row(g_ffn2), w2_in, w2_out)
    return out.reshape(batch, seq, d)


def kernel(x, mem, g_ffn1, w_ffn1_in, w_ffn1_out, g_mix, w_in, b_f, g_q, g_k, g_sgu, w_s, b_s, g_fox_o,
           g_gmlp_o, w_out, g_ca, g_mem, w_cq, w_ckv, g_cq, g_ck, w_co, g_ffn2, w_ffn2_in, w_ffn2_out):
    layer_params = (g_ffn1, w_ffn1_in, w_ffn1_out, g_mix, w_in, b_f, g_q, g_k, g_sgu, w_s, b_s, g_fox_o,
                    g_gmlp_o, w_out, g_ca, g_mem, w_cq, w_ckv, g_cq, g_ck, w_co, g_ffn2, w_ffn2_in, w_ffn2_out)
    for layer in range(g_ffn1.shape[0]):
        x = _layer(layer, x, mem, *layer_params)
    return x
```

```python
import functools

import jax
import jax.numpy as jnp
import numpy as np
from jax import lax
from jax.experimental import pallas as pl
from jax.experimental.pallas import tpu as pltpu

F32 = jnp.float32
BF16 = jnp.bfloat16
EPS = 1e-6
LOG2E = 1.4426950408889634
MASKED = -1e30

FOX_HEADS = 8
FOX_DH = 64
FOX_W = FOX_HEADS * FOX_DH
GMLP_GROUPS = 8
GMLP_DG = 64
GMLP_W = GMLP_GROUPS * GMLP_DG
CHUNK = 128
CA_HEADS = 4

LANES = 128
VMEM_LIMIT_BYTES = 56 * 1024 * 1024

TM = 512
TQ = 512
TK = 512
FC = 256
PAIRS = FOX_HEADS // 2
BIAS_ROWS = 32
V_ROWS = FOX_DH + 16


def _rms(x):
    return x * lax.rsqrt(jnp.mean(x * x, axis=-1, keepdims=True) + EPS)


def _dot(a, b):
    return lax.dot_general(a, b, (((1,), (0,)), ((), ())), preferred_element_type=F32)


def _dot_nt(a, b):
    return lax.dot_general(a, b, (((1,), (1,)), ((), ())), preferred_element_type=F32)


def _split3(x):
    hi = x.astype(BF16).astype(F32)
    mid = (x - hi).astype(BF16).astype(F32)
    lo = (x - hi - mid).astype(BF16).astype(F32)
    return hi, mid, lo


def _resident(shape):
    return pl.BlockSpec(shape, lambda *_: (0,) * len(shape), pipeline_mode=pl.Buffered(1))


def _params(n_axes):
    return pltpu.CompilerParams(dimension_semantics=("arbitrary",) * n_axes,
                                vmem_limit_bytes=VMEM_LIMIT_BYTES)


def _swiglu_half_step(x, g_ref, wgu_ref, wo_ref, act_ref):
    d_ff = wo_ref.shape[0]
    h = (_rms(x) * g_ref[...]).astype(BF16)
    for c in range(d_ff // FC):
        gate = _dot(h, wgu_ref[:, c * FC:(c + 1) * FC])
        up = _dot(h, wgu_ref[:, d_ff + c * FC:d_ff + (c + 1) * FC])
        act_ref[:, c * FC:(c + 1) * FC] = (gate * jax.nn.sigmoid(gate) * up).astype(BF16)
    return x + 0.5 * _dot(act_ref[...], wo_ref[...])


def _ffn_kernel(x_ref, g_ref, wgu_ref, wo_ref, *refs, n_cast):
    o_ref, act_ref = refs[n_cast], refs[-1]
    for src_ref, dst_ref in zip(refs[:n_cast], refs[n_cast + 1:-1]):
        dst_ref[...] = src_ref[0].astype(BF16)
    o_ref[...] = _swiglu_half_step(x_ref[...], g_ref, wgu_ref.at[0], wo_ref.at[0], act_ref)


def _cast_rows(rows, steps):
    return next(rb for rb in range(16, rows + 1, 16) if rows % rb == 0 and rows // rb <= steps)


def _ffn(x, g, wgu, wo, layer, to_bf16=()):
    t, d = x.shape
    d_ff = wo.shape[1]
    steps = t // TM
    row = pl.BlockSpec((TM, d), lambda r: (r, 0))

    def stacked(shape):
        return pl.BlockSpec((1,) + shape, lambda r: (layer, 0, 0), pipeline_mode=pl.Buffered(1))

    srcs, dsts = [], []
    for w in to_bf16:
        rb = _cast_rows(w.shape[1], steps)
        last = w.shape[1] // rb - 1
        srcs.append(pl.BlockSpec((1, rb, w.shape[2]), lambda r, last=last: (layer, jnp.minimum(r, last), 0)))
        dsts.append(pl.BlockSpec((rb, w.shape[2]), lambda r, last=last: (jnp.minimum(r, last), 0)))
    outs = pl.pallas_call(
        functools.partial(_ffn_kernel, n_cast=len(to_bf16)),
        grid=(steps,),
        in_specs=[row, _resident((1, d)), stacked((d, 2 * d_ff)), stacked((d_ff, d))] + srcs,
        out_specs=[row] + dsts,
        out_shape=[jax.ShapeDtypeStruct((t, d), F32)] + [jax.ShapeDtypeStruct(w.shape[1:], BF16) for w in to_bf16],
        scratch_shapes=[pltpu.VMEM((TM, d_ff), BF16)],
        compiler_params=_params(1),
        name="ffn",
    )(x, g, wgu, wo, *to_bf16)
    return outs[0], outs[1:]


def _gelu_tanh(x):
    c = float(np.sqrt(2.0 / np.pi))
    return x * (0.5 * jnp.tanh(x * (c + (c * 0.044715) * (x * x))) + 0.5)


def _proj_kernel(x_ref, g_mix_ref, w_nn_ref, w_nt_ref, gq_ref, gk_ref, bf_ref, headsum_ref, triu_ref,
                 place_ref, g_sgu_ref, ws_ref, bs_ref, g_go_ref,
                 qT_ref, cT_ref, k_ref, cb_ref, vT_ref, yg_ref,
                 carry_ref, sgu_ref, *, blocks_per_seq):
    @pl.when(pl.program_id(0) % blocks_per_seq == 0)
    def _():
        carry_ref[...] = jnp.zeros_like(carry_ref)

    h = (_rms(x_ref[...]) * g_mix_ref[...]).astype(BF16)

    zT = _dot_nt(w_nt_ref[...], h)
    for hd in range(FOX_HEADS):
        blk = zT[hd * FOX_DH:(hd + 1) * FOX_DH, :]
        inv = lax.rsqrt(jnp.mean(blk * blk, axis=0, keepdims=True) + EPS)
        qT_ref[0, hd * FOX_DH:(hd + 1) * FOX_DH, :] = (blk * inv * gq_ref[...]).astype(BF16)
    ones_rows = jnp.where(lax.broadcasted_iota(jnp.int32, (V_ROWS - FOX_DH, TK), 0) == 0, 1.0, 0.0).astype(BF16)
    for c in range(TM // TK):
        for hd in range(FOX_HEADS):
            vT_ref[0, c, hd * V_ROWS:hd * V_ROWS + FOX_DH, :] = (
                zT[FOX_W + hd * FOX_DH:FOX_W + (hd + 1) * FOX_DH, c * TK:(c + 1) * TK].astype(BF16))
            vT_ref[0, c, hd * V_ROWS + FOX_DH:(hd + 1) * V_ROWS, :] = ones_rows

    f = zT[2 * FOX_W:2 * FOX_W + FOX_HEADS, :] + bf_ref[...]
    logf = (jnp.minimum(f, 0.0) - jnp.log1p(jnp.exp(-jnp.abs(f)))) * LOG2E
    zeros8 = jnp.zeros((8, TM), F32)
    parts = jnp.concatenate(_split3(logf) + (zeros8,), axis=0).astype(BF16)
    cs = _dot(parts, triu_ref[...])
    c = cs[0:8] + cs[8:16] + cs[16:24] + carry_ref[:, 0:1]
    carry_ref[...] = jnp.broadcast_to(c[:, TM - 1:TM], carry_ref.shape)
    cT_ref[0] = c

    one_row = jnp.where(lax.broadcasted_iota(jnp.int32, (8, TM), 0) == 0, 1.0, 0.0)
    cparts = jnp.concatenate(_split3(c) + (one_row,), axis=0).astype(BF16)
    cb_ref[...] = _dot(place_ref[...], cparts).T.astype(BF16)

    yg_ref[...] = _gmlp_branch(h, w_nn_ref, g_sgu_ref, ws_ref, bs_ref, g_go_ref, sgu_ref)

    k = _dot(h, w_nn_ref[:, :FOX_W])
    ksum = _dot((k * k).astype(BF16), headsum_ref[...])
    k_ref[...] = (k * lax.rsqrt(ksum * (1.0 / FOX_DH) + EPS) * gk_ref[...]).astype(BF16)


def _gmlp_branch(h, w_nn_ref, g_sgu_ref, ws_ref, bs_ref, g_go_ref, sgu_ref):
    vg = _gelu_tanh(_dot(h, w_nn_ref[:, FOX_W + GMLP_W:]))
    vgn = (_rms(vg) * g_sgu_ref[...]).astype(BF16)
    u = _gelu_tanh(_dot(h, w_nn_ref[:, FOX_W:FOX_W + GMLP_W]))
    lane = lax.broadcasted_iota(jnp.int32, (CHUNK, LANES), 1)
    tril = (lax.broadcasted_iota(jnp.int32, (CHUNK, CHUNK), 0)
            >= lax.broadcasted_iota(jnp.int32, (CHUNK, CHUNK), 1))
    n_chunks = TM // CHUNK
    for p in range(GMLP_GROUPS // 2):
        lo_blocks, hi_blocks = [], []
        for ch in range(n_chunks):
            vp = vgn[ch * CHUNK:(ch + 1) * CHUNK, p * LANES:(p + 1) * LANES]
            lo_blocks.append(jnp.where(lane < GMLP_DG, vp, jnp.zeros_like(vp)))
            hi_blocks.append(jnp.where(lane >= GMLP_DG, vp, jnp.zeros_like(vp)))
        vcat = jnp.concatenate([jnp.concatenate(lo_blocks, axis=1),
                                jnp.concatenate(hi_blocks, axis=1)], axis=0)
        w0 = ws_ref[2 * p]
        w1 = ws_ref[2 * p + 1]
        wpair = jnp.concatenate([jnp.where(tril, w0, jnp.zeros_like(w0)),
                                 jnp.where(tril, w1, jnp.zeros_like(w1))], axis=1)
        mixed = _dot(wpair, vcat)
        bias = bs_ref[:, p * LANES:(p + 1) * LANES]
        for ch in range(n_chunks):
            sgu_ref[ch * CHUNK:(ch + 1) * CHUNK, p * LANES:(p + 1) * LANES] = (
                u[ch * CHUNK:(ch + 1) * CHUNK, p * LANES:(p + 1) * LANES]
                * (mixed[:, ch * LANES:(ch + 1) * LANES] + bias))
    return (_rms(sgu_ref[...]) * g_go_ref[...]).astype(BF16)


def _proj(x, batch, seq, g_mix, w_nn, w_nt, gq, gk, bfc, headsum, triu, place, g_sgu, ws, bs, g_go):
    t, d = x.shape
    bps = seq // TM
    n_nt = w_nt.shape[0]

    def rows(width):
        return pl.BlockSpec((TM, width), lambda r: (r, 0))

    kern = functools.partial(_proj_kernel, blocks_per_seq=bps)
    return pl.pallas_call(
        kern,
        grid=(t // TM,),
        in_specs=[rows(d), _resident((1, d)), _resident((d, w_nn.shape[1])), _resident((n_nt, d)),
                  _resident((FOX_DH, TM)), _resident((1, FOX_W)), _resident((FOX_HEADS, TM)),
                  _resident((FOX_W, FOX_W)), _resident((TM, TM)), _resident((PAIRS * BIAS_ROWS, 32)),
                  _resident((1, GMLP_W)), _resident((GMLP_GROUPS, CHUNK, CHUNK)),
                  _resident((CHUNK, GMLP_W)), _resident((1, GMLP_W))],
        out_specs=[pl.BlockSpec((1, FOX_W, TM), lambda r: (r // bps, 0, r % bps)),
                   pl.BlockSpec((1, FOX_HEADS, TM), lambda r: (r // bps, 0, r % bps)),
                   rows(FOX_W), rows(LANES),
                   pl.BlockSpec((1, TM // TK, FOX_HEADS * V_ROWS, TK), lambda r: (r // bps, r % bps, 0, 0)),
                   rows(GMLP_W)],
        out_shape=[jax.ShapeDtypeStruct((batch, FOX_W, seq), BF16),
                   jax.ShapeDtypeStruct((batch, FOX_HEADS, seq), F32),
                   jax.ShapeDtypeStruct((t, FOX_W), BF16),
                   jax.ShapeDtypeStruct((t, LANES), BF16),
                   jax.ShapeDtypeStruct((batch, seq // TK, FOX_HEADS * V_ROWS, TK), BF16),
                   jax.ShapeDtypeStruct((t, GMLP_W), BF16)],
        scratch_shapes=[pltpu.VMEM((FOX_HEADS, LANES), F32), pltpu.VMEM((TM, GMLP_W), F32)],
        compiler_params=_params(1),
        name="proj",
    )(x, g_mix, w_nn, w_nt, gq, gk, bfc, headsum, triu, place, g_sgu, ws, bs, g_go)


def _fox_kernel(qT_ref, cT_ref, k_ref, cb_ref, vT_ref, o_ref, qq_ref, s_ref, acc_ref):
    i = pl.program_id(1)

    row = lax.broadcasted_iota(jnp.int32, (BIAS_ROWS, TQ), 0)
    zeros_q = jnp.zeros((FOX_DH, TQ), BF16)
    for p in range(PAIRS):
        for e in range(2):
            hd = 2 * p + e
            qh = qT_ref[0, hd * FOX_DH:(hd + 1) * FOX_DH, :]
            hi, mid, lo = _split3(cT_ref[0, hd:hd + 1, :])
            blk = jnp.where((row >= 3 * e) & (row < 3 * e + 3), 1.0, 0.0)
            blk = jnp.where(row == 6, hi, blk)
            blk = jnp.where(row == 7, mid, blk)
            blk = jnp.where(row == 8, lo, blk)
            pieces = [qh, zeros_q] if e == 0 else [zeros_q, qh]
            if p > 0:
                pieces.append(jnp.zeros((BIAS_ROWS * p, TQ), BF16))
            pieces.append(blk.astype(BF16))
            if p + 1 < PAIRS:
                pieces.append(jnp.zeros((BIAS_ROWS * (PAIRS - 1 - p), TQ), BF16))
            qq_ref[p, :, e * TQ:(e + 1) * TQ] = jnp.concatenate(pieces, axis=0)

    def key_chunk(p, j):
        k0 = pl.multiple_of(j * TK, TK)
        return jnp.concatenate([k_ref[0, pl.ds(k0, TK), p * LANES:(p + 1) * LANES],
                                cb_ref[0, pl.ds(k0, TK), :]], axis=1)

    def scores(p, j):
        s = _dot(key_chunk(p, j), qq_ref[p])
        s_ref[p] = s
        return jnp.max(s, axis=0, keepdims=True)

    def absorb(p, j, smax, m_old):
        m = jnp.maximum(m_old, smax)
        alpha = jnp.exp2(m_old - m)
        pb = jnp.exp2(s_ref[p] - m).astype(BF16)
        for e in range(2):
            hd = 2 * p + e
            pv = _dot(vT_ref[0, j, hd * V_ROWS:(hd + 1) * V_ROWS, :], pb[:, e * TQ:(e + 1) * TQ])
            acc_ref[hd] = alpha[:, e * TQ:(e + 1) * TQ] * acc_ref[hd] + pv
        return m

    H = TQ // 2
    future = lax.broadcasted_iota(jnp.int32, (H, H), 0) > lax.broadcasted_iota(jnp.int32, (H, H), 1)

    def late(x):
        return jnp.concatenate([x[:, H:TQ], x[:, TQ + H:]], axis=1)

    def scores_diag(p):
        kk = key_chunk(p, i)
        qq = qq_ref[p]
        s_early = _dot(kk[:H], qq)
        s_late = _dot(kk[H:], late(qq))
        tops_max = []
        for e in range(2):
            sq = jnp.where(future, MASKED, s_early[:, e * TQ:e * TQ + H])
            s_ref[p, 0:H, e * TQ:e * TQ + H] = sq
            s_ref[p, 0:H, e * TQ + H:(e + 1) * TQ] = s_early[:, e * TQ + H:(e + 1) * TQ]
            lq = jnp.where(future, MASKED, s_late[:, e * H:(e + 1) * H])
            s_ref[p, H:2 * H, e * H:(e + 1) * H] = lq
            tops_max += [jnp.max(sq, axis=0, keepdims=True),
                         jnp.maximum(jnp.max(s_early[:, e * TQ + H:(e + 1) * TQ], axis=0, keepdims=True),
                                     jnp.max(lq, axis=0, keepdims=True))]
        return jnp.concatenate(tops_max, axis=1)

    def absorb_diag(p, m):
        p_early = jnp.exp2(s_ref[p, 0:H, :] - m).astype(BF16)
        p_late = jnp.exp2(s_ref[p, H:2 * H, 0:TQ] - late(m)).astype(BF16)
        for e in range(2):
            hd = 2 * p + e
            vt = vT_ref[0, i, hd * V_ROWS:(hd + 1) * V_ROWS, :]
            pv_early = _dot(vt[:, :H], p_early[:, e * TQ:(e + 1) * TQ])
            pv_late = _dot(vt[:, H:], p_late[:, e * H:(e + 1) * H])
            acc_ref[hd] = jnp.concatenate([pv_early[:, :H], pv_early[:, H:] + pv_late], axis=1)
        return m

    def chunk(j, carry):
        smax, ms = carry
        new_m = []
        for p in range(PAIRS):
            nxt_smax = scores(p + 1, j) if p + 1 < PAIRS else scores(0, j + 1)
            new_m.append(absorb(p, j, smax, ms[p]))
            smax = nxt_smax
        return smax, tuple(new_m)

    ms = []
    smax = scores_diag(0)
    for p in range(PAIRS):
        nxt_smax = scores_diag(p + 1) if p + 1 < PAIRS else scores(0, 0)
        ms.append(absorb_diag(p, smax))
        smax = nxt_smax
    carry = (smax, tuple(ms))
    lax.fori_loop(0, i, chunk, carry)

    outs = []
    for hd in range(FOX_HEADS):
        acc = acc_ref[hd]
        outs.append(acc[:FOX_DH] * (1.0 / acc[FOX_DH:FOX_DH + 1]))
    o_ref[0] = jnp.concatenate(outs, axis=0).T


def _fox(qT, cT, k, cb, vT):
    batch, _, seq = qT.shape
    return pl.pallas_call(
        _fox_kernel,
        grid=(batch, seq // TQ),
        in_specs=[pl.BlockSpec((1, FOX_W, TQ), lambda b, i: (b, 0, i)),
                  pl.BlockSpec((1, FOX_HEADS, TQ), lambda b, i: (b, 0, i)),
                  pl.BlockSpec((1, seq, FOX_W), lambda b, i: (b, 0, 0)),
                  pl.BlockSpec((1, seq, LANES), lambda b, i: (b, 0, 0)),
                  pl.BlockSpec((1, seq // TK, FOX_HEADS * V_ROWS, TK), lambda b, i: (b, 0, 0, 0))],
        out_specs=pl.BlockSpec((1, TQ, FOX_W), lambda b, i: (b, i, 0)),
        out_shape=jax.ShapeDtypeStruct((batch, seq, FOX_W), F32),
        scratch_shapes=[pltpu.VMEM((PAIRS, 2 * LANES, 2 * TQ), BF16),
                        pltpu.VMEM((PAIRS, TK, 2 * TQ), F32),
                        pltpu.VMEM((FOX_HEADS, V_ROWS, TQ), F32)],
        compiler_params=_params(2),
        name="fox",
    )(qT, cT, k, cb, vT)


def _memkv_kernel(mem_ref, g_mem_ref, w_ckT_ref, w_cv_ref, g_ck_ref, kcT_ref, vc_ref):
    mn = (_rms(mem_ref[0]) * g_mem_ref[...]).astype(BF16)
    kT = _dot_nt(w_ckT_ref[...], mn)
    dh = kT.shape[0] // CA_HEADS
    for hd in range(CA_HEADS):
        blk = kT[hd * dh:(hd + 1) * dh, :]
        inv = lax.rsqrt(jnp.mean(blk * blk, axis=0, keepdims=True) + EPS)
        kcT_ref[0, hd * dh:(hd + 1) * dh, :] = (blk * inv * g_ck_ref[...]).astype(BF16)
    vc_ref[0] = _dot(mn, w_cv_ref[...]).astype(BF16)


def _memkv(mem, g_mem, w_ckT, w_cv, g_ck_col):
    batch, m, d = mem.shape
    return pl.pallas_call(
        _memkv_kernel,
        grid=(batch,),
        in_specs=[pl.BlockSpec((1, m, d), lambda b: (b, 0, 0)), _resident((1, d)), _resident((d, d)),
                  _resident((d, d)), _resident((d // CA_HEADS, m))],
        out_specs=[pl.BlockSpec((1, d, m), lambda b: (b, 0, 0)), pl.BlockSpec((1, m, d), lambda b: (b, 0, 0))],
        out_shape=[jax.ShapeDtypeStruct((batch, d, m), BF16), jax.ShapeDtypeStruct((batch, m, d), BF16)],
        compiler_params=_params(1),
        name="memkv",
    )(mem, g_mem, w_ckT, w_cv, g_ck_col)


def _tail_kernel(x_ref, attn_ref, yg_ref, g_fo_ref, w_out_ref, g_ca_ref, w_cq_ref, g_cq_ref, kcT_ref, vc_ref,
                 w_co_ref, g_f2_ref, wgu_ref, wo_ref, o_ref, act_ref):
    a = (_rms(attn_ref[...]) * g_fo_ref[...]).astype(BF16)
    x2 = x_ref[...] + _dot(jnp.concatenate([a, yg_ref[...]], axis=1), w_out_ref[...])

    hq = (_rms(x2) * g_ca_ref[...]).astype(BF16)
    qc = _dot(hq, w_cq_ref[...])
    dh = qc.shape[1] // CA_HEADS
    outs = []
    for hd in range(CA_HEADS):
        qh = (_rms(qc[:, hd * dh:(hd + 1) * dh]) * g_cq_ref[...]).astype(BF16)
        s = _dot(qh, kcT_ref[0, hd * dh:(hd + 1) * dh, :])
        pexp = jnp.exp2(s - jnp.max(s, axis=-1, keepdims=True))
        denom = jnp.sum(pexp, axis=-1, keepdims=True)
        o = _dot(pexp.astype(BF16), vc_ref[0, :, hd * dh:(hd + 1) * dh]) * (1.0 / denom)
        outs.append(o.astype(BF16))
    x3 = x2 + _dot(jnp.concatenate(outs, axis=1), w_co_ref[...])

    o_ref[...] = _swiglu_half_step(x3, g_f2_ref, wgu_ref, wo_ref, act_ref)


def _tail(x1, attn, yg, seq, g_fo, w_out, g_ca, w_cq, g_cq, kcT, vc, w_co, g_f2, wgu, wo):
    t, d = x1.shape
    d_ff = wo.shape[0]
    m = vc.shape[1]
    bps = seq // TM

    def rows(width):
        return pl.BlockSpec((TM, width), lambda r: (r, 0))

    return pl.pallas_call(
        _tail_kernel,
        grid=(t // TM,),
        in_specs=[rows(d), rows(FOX_W), rows(GMLP_W), _resident((1, FOX_W)), _resident((d, d)),
                  _resident((1, d)), _resident((d, d)), _resident((1, d // CA_HEADS)),
                  pl.BlockSpec((1, d, m), lambda r: (r // bps, 0, 0)),
                  pl.BlockSpec((1, m, d), lambda r: (r // bps, 0, 0)),
                  _resident((d, d)), _resident((1, d)), _resident((d, 2 * d_ff)), _resident((d_ff, d))],
        out_specs=rows(d),
        out_shape=jax.ShapeDtypeStruct((t, d), F32),
        scratch_shapes=[pltpu.VMEM((TM, d_ff), BF16)],
        compiler_params=_params(1),
        name="tail",
    )(x1, attn, yg, g_fo, w_out, g_ca, w_cq, g_cq, kcT, vc, w_co, g_f2, wgu, wo)


def _placement():
    pm = np.zeros((PAIRS * BIAS_ROWS, 32), np.float32)
    for p in range(PAIRS):
        for e in range(2):
            for part in range(3):
                pm[BIAS_ROWS * p + 3 * e + part, 8 * part + 2 * p + e] = -1.0
        pm[BIAS_ROWS * p + 6:BIAS_ROWS * p + 9, 24] = 1.0
    return pm


def _layer(layer, x, mem, g_ffn1, w_ffn1_in, w_ffn1_out, g_mix, w_in, b_f, g_q, g_k, g_sgu, w_s, b_s, g_fox_o,
           g_gmlp_o, w_out, g_ca, g_mem, w_cq, w_ckv, g_cq, g_ck, w_co, g_ffn2, w_ffn2_in, w_ffn2_out):
    (g_ffn1, g_mix, w_in, b_f, g_q, g_k, g_sgu, w_s, b_s, g_fox_o, g_gmlp_o, g_ca, g_mem, w_ckv, g_cq, g_ck,
     g_ffn2) = (v[layer] for v in (g_ffn1, g_mix, w_in, b_f, g_q, g_k, g_sgu, w_s, b_s, g_fox_o, g_gmlp_o, g_ca,
                                   g_mem, w_ckv, g_cq, g_ck, g_ffn2))
    batch, seq, d = x.shape
    t = batch * seq
    m = mem.shape[1]
    dh_ca = d // CA_HEADS

    def row(v):
        return v.reshape(1, -1).astype(F32)

    k_off, v_off, f_off, uv_off = FOX_W, 2 * FOX_W, 3 * FOX_W, 3 * FOX_W + FOX_HEADS
    w_nn = jnp.concatenate([w_in[:, k_off:v_off], w_in[:, uv_off:]], axis=1).astype(BF16)
    w_nt = jnp.concatenate([w_in[:, :k_off].T, w_in[:, v_off:f_off].T, w_in[:, f_off:uv_off].T,
                            jnp.zeros((8, d), F32)], axis=0).astype(BF16)
    gq = jnp.broadcast_to((g_q * (FOX_DH ** -0.5 * LOG2E))[:, None], (FOX_DH, TM)).astype(F32)
    gk = row(jnp.tile(g_k, FOX_HEADS))
    bfc = jnp.broadcast_to(b_f[:, None], (FOX_HEADS, TM)).astype(F32)
    headsum = jnp.asarray(np.kron(np.eye(FOX_HEADS, dtype=np.float32),
                                  np.ones((FOX_DH, FOX_DH), np.float32)), BF16)
    triu = jnp.asarray(np.triu(np.ones((TM, TM), np.float32)), BF16)
    place = jnp.asarray(_placement(), BF16)
    bs = jnp.repeat(b_s.T, GMLP_DG, axis=1).astype(F32)
    g_cq_s = row(g_cq * (dh_ca ** -0.5 * LOG2E))
    g_ck_col = jnp.broadcast_to(g_ck[:, None], (dh_ca, m)).astype(F32)

    kcT, vc = _memkv(mem, row(g_mem), w_ckv[:, :d].T.astype(BF16), w_ckv[:, d:].astype(BF16), g_ck_col)

    x1, (w2_in, w2_out, w_out_b, w_cq_b, w_co_b) = _ffn(
        x.reshape(t, d), row(g_ffn1), w_ffn1_in, w_ffn1_out, layer, (w_ffn2_in, w_ffn2_out, w_out, w_cq, w_co))
    qT, cT, k, cb, vT, yg = _proj(x1, batch, seq, row(g_mix), w_nn, w_nt, gq, gk, bfc, headsum, triu, place,
                                  row(g_sgu), w_s.astype(BF16), bs, row(g_gmlp_o))
    attn = _fox(qT, cT, k.reshape(batch, seq, FOX_W), cb.reshape(batch, seq, LANES), vT)
    out = _tail(x1, attn.reshape(t, FOX_W), yg, seq, row(g_fox_o), w_out_b, row(g_ca),
                w_cq_b, g_cq_s, kcT, vc, w_co_b, row(g_ffn2), w2_in, w2_out)
    return out.reshape(batch, seq, d)


def kernel(x, mem, g_ffn1, w_ffn1_in, w_ffn1_out, g_mix, w_in, b_f, g_q, g_k, g_sgu, w_s, b_s, g_fox_o,
           g_gmlp_o, w_out, g_ca, g_mem, w_cq, w_ckv, g_cq, g_ck, w_co, g_ffn2, w_ffn2_in, w_ffn2_out):
    layer_params = (g_ffn1, w_ffn1_in, w_ffn1_out, g_mix, w_in, b_f, g_q, g_k, g_sgu, w_s, b_s, g_fox_o,
                    g_gmlp_o, w_out, g_ca, g_mem, w_cq, w_ckv, g_cq, g_ck, w_co, g_ffn2, w_ffn2_in, w_ffn2_out)
    for layer in range(g_ffn1.shape[0]):
        x = _layer(layer, x, mem, *layer_params)
    return x
```

```python
import functools

import jax
import jax.numpy as jnp
import numpy as np
from jax import lax
from jax.experimental import pallas as pl
from jax.experimental.pallas import tpu as pltpu

F32 = jnp.float32
BF16 = jnp.bfloat16
EPS = 1e-6
LOG2E = 1.4426950408889634
MASKED = -1e30

FOX_HEADS = 8
FOX_DH = 64
FOX_W = FOX_HEADS * FOX_DH
GMLP_GROUPS = 8
GMLP_DG = 64
GMLP_W = GMLP_GROUPS * GMLP_DG
CHUNK = 128
CA_HEADS = 4

LANES = 128
VMEM_LIMIT_BYTES = 56 * 1024 * 1024

TM = 512
TQ = 512
TK = 512
FC = 256
PAIRS = FOX_HEADS // 2
BIAS_ROWS = 32
V_ROWS = FOX_DH + 16
K_OFF, V_OFF, F_OFF, UV_OFF = FOX_W, 2 * FOX_W, 3 * FOX_W, 3 * FOX_W + FOX_HEADS
NT_ROWS = 2 * FOX_W + 16
W_IN_COLS_PER_STEP = LANES


def _rms(x):
    return x * lax.rsqrt(jnp.mean(x * x, axis=-1, keepdims=True) + EPS)


def _dot(a, b):
    return lax.dot_general(a, b, (((1,), (0,)), ((), ())), preferred_element_type=F32)


def _dot_nt(a, b):
    return lax.dot_general(a, b, (((1,), (1,)), ((), ())), preferred_element_type=F32)


def _split3(x):
    hi = x.astype(BF16).astype(F32)
    mid = (x - hi).astype(BF16).astype(F32)
    lo = (x - hi - mid).astype(BF16).astype(F32)
    return hi, mid, lo


def _resident(shape):
    return pl.BlockSpec(shape, lambda *_: (0,) * len(shape), pipeline_mode=pl.Buffered(1))


def _params(n_axes):
    return pltpu.CompilerParams(dimension_semantics=("arbitrary",) * n_axes,
                                vmem_limit_bytes=VMEM_LIMIT_BYTES)


def _swiglu_half_step(x, g_ref, wgu_ref, wo_ref, act_ref):
    d_ff = wo_ref.shape[0]
    h = (_rms(x) * g_ref[...]).astype(BF16)
    for c in range(d_ff // FC):
        gate = _dot(h, wgu_ref[:, c * FC:(c + 1) * FC])
        up = _dot(h, wgu_ref[:, d_ff + c * FC:d_ff + (c + 1) * FC])
        act_ref[:, c * FC:(c + 1) * FC] = (gate * jax.nn.sigmoid(gate) * up).astype(BF16)
    return x + 0.5 * _dot(act_ref[...], wo_ref[...])


def _ffn_kernel(x_ref, g_ref, wgu_ref, wo_ref, *refs, n_cast):
    srcs, (w_rows_ref, w_cols_ref) = refs[:n_cast], refs[n_cast:n_cast + 2]
    o_ref, dsts = refs[n_cast + 2], refs[n_cast + 3:2 * n_cast + 3]
    wnn_ref, wnt_ref, act_ref = refs[2 * n_cast + 3:]
    for src_ref, dst_ref in zip(srcs, dsts):
        dst_ref[...] = src_ref[0].astype(BF16)
    slab = w_rows_ref[0]
    wnn_ref[...] = jnp.concatenate([slab[:, K_OFF:V_OFF], slab[:, UV_OFF:]], axis=1).astype(BF16)
    wide = w_cols_ref[0]
    f_t = wide[:, F_OFF:F_OFF + LANES].T[:FOX_HEADS]
    wnt_ref[...] = jnp.concatenate([wide[:, :K_OFF].T, wide[:, V_OFF:F_OFF].T, f_t,
                                    jnp.zeros((NT_ROWS - 2 * FOX_W - FOX_HEADS, LANES), F32)], axis=0).astype(BF16)
    o_ref[...] = _swiglu_half_step(x_ref[...], g_ref, wgu_ref.at[0], wo_ref.at[0], act_ref)


def _cast_rows(rows, steps):
    return next(rb for rb in range(16, rows + 1, 16) if rows % rb == 0 and rows // rb <= steps)


def _ffn(x, g, wgu, wo, layer, to_bf16, w_in):
    t, d = x.shape
    d_ff = wo.shape[1]
    steps = t // TM
    row = pl.BlockSpec((TM, d), lambda r: (r, 0))

    def stacked(shape):
        return pl.BlockSpec((1,) + shape, lambda r: (layer, 0, 0), pipeline_mode=pl.Buffered(1))

    srcs, dsts = [], []
    for w in to_bf16:
        rb = _cast_rows(w.shape[1], steps)
        last = w.shape[1] // rb - 1
        srcs.append(pl.BlockSpec((1, rb, w.shape[2]), lambda r, last=last: (layer, jnp.minimum(r, last), 0)))
        dsts.append(pl.BlockSpec((rb, w.shape[2]), lambda r, last=last: (jnp.minimum(r, last), 0)))
    in_cols = w_in.shape[2]
    nn_cols = in_cols - UV_OFF + FOX_W
    last_t = d // W_IN_COLS_PER_STEP - 1
    srcs += [pl.BlockSpec((1, d // steps, in_cols), lambda r: (layer, r, 0)),
             pl.BlockSpec((1, W_IN_COLS_PER_STEP, in_cols), lambda r: (layer, jnp.minimum(r, last_t), 0))]
    dsts += [pl.BlockSpec((d // steps, nn_cols), lambda r: (r, 0)),
             pl.BlockSpec((NT_ROWS, W_IN_COLS_PER_STEP), lambda r: (0, jnp.minimum(r, last_t)))]
    outs = pl.pallas_call(
        functools.partial(_ffn_kernel, n_cast=len(to_bf16)),
        grid=(steps,),
        in_specs=[row, _resident((1, d)), stacked((d, 2 * d_ff)), stacked((d_ff, d))] + srcs,
        out_specs=[row] + dsts,
        out_shape=([jax.ShapeDtypeStruct((t, d), F32)] + [jax.ShapeDtypeStruct(w.shape[1:], BF16) for w in to_bf16]
                   + [jax.ShapeDtypeStruct((d, nn_cols), BF16), jax.ShapeDtypeStruct((NT_ROWS, d), BF16)]),
        scratch_shapes=[pltpu.VMEM((TM, d_ff), BF16)],
        compiler_params=_params(1),
        name="ffn",
    )(x, g, wgu, wo, *to_bf16, w_in, w_in)
    return outs[0], outs[1:-2], outs[-2], outs[-1]


def _gelu_tanh(x):
    c = float(np.sqrt(2.0 / np.pi))
    return x * (0.5 * jnp.tanh(x * (c + (c * 0.044715) * (x * x))) + 0.5)


def _proj_kernel(x_ref, g_mix_ref, w_nn_ref, w_nt_ref, gq_ref, gk_ref, bf_ref, headsum_ref, triu_ref,
                 place_ref, g_sgu_ref, ws_ref, bs_ref, g_go_ref,
                 qT_ref, cT_ref, k_ref, cb_ref, vT_ref, yg_ref,
                 carry_ref, sgu_ref, *, blocks_per_seq):
    @pl.when(pl.program_id(0) % blocks_per_seq == 0)
    def _():
        carry_ref[...] = jnp.zeros_like(carry_ref)

    h = (_rms(x_ref[...]) * g_mix_ref[...]).astype(BF16)

    zT = _dot_nt(w_nt_ref[...], h)
    for hd in range(FOX_HEADS):
        blk = zT[hd * FOX_DH:(hd + 1) * FOX_DH, :]
        inv = lax.rsqrt(jnp.mean(blk * blk, axis=0, keepdims=True) + EPS)
        qT_ref[0, hd * FOX_DH:(hd + 1) * FOX_DH, :] = (blk * inv * gq_ref[...]).astype(BF16)
    ones_rows = jnp.where(lax.broadcasted_iota(jnp.int32, (V_ROWS - FOX_DH, TK), 0) == 0, 1.0, 0.0).astype(BF16)
    for c in range(TM // TK):
        for hd in range(FOX_HEADS):
            vT_ref[0, c, hd * V_ROWS:hd * V_ROWS + FOX_DH, :] = (
                zT[FOX_W + hd * FOX_DH:FOX_W + (hd + 1) * FOX_DH, c * TK:(c + 1) * TK].astype(BF16))
            vT_ref[0, c, hd * V_ROWS + FOX_DH:(hd + 1) * V_ROWS, :] = ones_rows

    f = zT[2 * FOX_W:2 * FOX_W + FOX_HEADS, :] + bf_ref[...]
    logf = (jnp.minimum(f, 0.0) - jnp.log1p(jnp.exp(-jnp.abs(f)))) * LOG2E
    zeros8 = jnp.zeros((8, TM), F32)
    parts = jnp.concatenate(_split3(logf) + (zeros8,), axis=0).astype(BF16)
    cs = _dot(parts, triu_ref[...])
    c = cs[0:8] + cs[8:16] + cs[16:24] + carry_ref[:, 0:1]
    carry_ref[...] = jnp.broadcast_to(c[:, TM - 1:TM], carry_ref.shape)
    cT_ref[0] = c

    one_row = jnp.where(lax.broadcasted_iota(jnp.int32, (8, TM), 0) == 0, 1.0, 0.0)
    cparts = jnp.concatenate(_split3(c) + (one_row,), axis=0).astype(BF16)
    cb_ref[...] = _dot(place_ref[...], cparts).T.astype(BF16)

    yg_ref[...] = _gmlp_branch(h, w_nn_ref, g_sgu_ref, ws_ref, bs_ref, g_go_ref, sgu_ref)

    k = _dot(h, w_nn_ref[:, :FOX_W])
    ksum = _dot((k * k).astype(BF16), headsum_ref[...])
    k_ref[...] = (k * lax.rsqrt(ksum * (1.0 / FOX_DH) + EPS) * gk_ref[...]).astype(BF16)


def _gmlp_branch(h, w_nn_ref, g_sgu_ref, ws_ref, bs_ref, g_go_ref, sgu_ref):
    vg = _gelu_tanh(_dot(h, w_nn_ref[:, FOX_W + GMLP_W:]))
    vgn = (_rms(vg) * g_sgu_ref[...]).astype(BF16)
    u = _gelu_tanh(_dot(h, w_nn_ref[:, FOX_W:FOX_W + GMLP_W]))
    lane = lax.broadcasted_iota(jnp.int32, (CHUNK, LANES), 1)
    tril = (lax.broadcasted_iota(jnp.int32, (CHUNK, CHUNK), 0)
            >= lax.broadcasted_iota(jnp.int32, (CHUNK, CHUNK), 1))
    n_chunks = TM // CHUNK
    for p in range(GMLP_GROUPS // 2):
        lo_blocks, hi_blocks = [], []
        for ch in range(n_chunks):
            vp = vgn[ch * CHUNK:(ch + 1) * CHUNK, p * LANES:(p + 1) * LANES]
            lo_blocks.append(jnp.where(lane < GMLP_DG, vp, jnp.zeros_like(vp)))
            hi_blocks.append(jnp.where(lane >= GMLP_DG, vp, jnp.zeros_like(vp)))
        vcat = jnp.concatenate([jnp.concatenate(lo_blocks, axis=1),
                                jnp.concatenate(hi_blocks, axis=1)], axis=0)
        w0 = ws_ref[2 * p]
        w1 = ws_ref[2 * p + 1]
        wpair = jnp.concatenate([jnp.where(tril, w0, jnp.zeros_like(w0)),
                                 jnp.where(tril, w1, jnp.zeros_like(w1))], axis=1)
        mixed = _dot(wpair, vcat)
        bias = bs_ref[:, p * LANES:(p + 1) * LANES]
        for ch in range(n_chunks):
            sgu_ref[ch * CHUNK:(ch + 1) * CHUNK, p * LANES:(p + 1) * LANES] = (
                u[ch * CHUNK:(ch + 1) * CHUNK, p * LANES:(p + 1) * LANES]
                * (mixed[:, ch * LANES:(ch + 1) * LANES] + bias))
    return (_rms(sgu_ref[...]) * g_go_ref[...]).astype(BF16)


def _proj(x, batch, seq, g_mix, w_nn, w_nt, gq, gk, bfc, headsum, triu, place, g_sgu, ws, bs, g_go):
    t, d = x.shape
    bps = seq // TM
    n_nt = w_nt.shape[0]

    def rows(width):
        return pl.BlockSpec((TM, width), lambda r: (r, 0))

    kern = functools.partial(_proj_kernel, blocks_per_seq=bps)
    return pl.pallas_call(
        kern,
        grid=(t // TM,),
        in_specs=[rows(d), _resident((1, d)), _resident((d, w_nn.shape[1])), _resident((n_nt, d)),
                  _resident((FOX_DH, TM)), _resident((1, FOX_W)), _resident((FOX_HEADS, TM)),
                  _resident((FOX_W, FOX_W)), _resident((TM, TM)), _resident((PAIRS * BIAS_ROWS, 32)),
                  _resident((1, GMLP_W)), _resident((GMLP_GROUPS, CHUNK, CHUNK)),
                  _resident((CHUNK, GMLP_W)), _resident((1, GMLP_W))],
        out_specs=[pl.BlockSpec((1, FOX_W, TM), lambda r: (r // bps, 0, r % bps)),
                   pl.BlockSpec((1, FOX_HEADS, TM), lambda r: (r // bps, 0, r % bps)),
                   rows(FOX_W), rows(LANES),
                   pl.BlockSpec((1, TM // TK, FOX_HEADS * V_ROWS, TK), lambda r: (r // bps, r % bps, 0, 0)),
                   rows(GMLP_W)],
        out_shape=[jax.ShapeDtypeStruct((batch, FOX_W, seq), BF16),
                   jax.ShapeDtypeStruct((batch, FOX_HEADS, seq), F32),
                   jax.ShapeDtypeStruct((t, FOX_W), BF16),
                   jax.ShapeDtypeStruct((t, LANES), BF16),
                   jax.ShapeDtypeStruct((batch, seq // TK, FOX_HEADS * V_ROWS, TK), BF16),
                   jax.ShapeDtypeStruct((t, GMLP_W), BF16)],
        scratch_shapes=[pltpu.VMEM((FOX_HEADS, LANES), F32), pltpu.VMEM((TM, GMLP_W), F32)],
        compiler_params=_params(1),
        name="proj",
    )(x, g_mix, w_nn, w_nt, gq, gk, bfc, headsum, triu, place, g_sgu, ws, bs, g_go)


def _fox_kernel(qT_ref, cT_ref, k_ref, cb_ref, vT_ref, o_ref, qq_ref, s_ref, acc_ref):
    i = pl.program_id(1)

    row = lax.broadcasted_iota(jnp.int32, (BIAS_ROWS, TQ), 0)
    zeros_q = jnp.zeros((FOX_DH, TQ), BF16)
    for p in range(PAIRS):
        for e in range(2):
            hd = 2 * p + e
            qh = qT_ref[0, hd * FOX_DH:(hd + 1) * FOX_DH, :]
            hi, mid, lo = _split3(cT_ref[0, hd:hd + 1, :])
            blk = jnp.where((row >= 3 * e) & (row < 3 * e + 3), 1.0, 0.0)
            blk = jnp.where(row == 6, hi, blk)
            blk = jnp.where(row == 7, mid, blk)
            blk = jnp.where(row == 8, lo, blk)
            pieces = [qh, zeros_q] if e == 0 else [zeros_q, qh]
            if p > 0:
                pieces.append(jnp.zeros((BIAS_ROWS * p, TQ), BF16))
            pieces.append(blk.astype(BF16))
            if p + 1 < PAIRS:
                pieces.append(jnp.zeros((BIAS_ROWS * (PAIRS - 1 - p), TQ), BF16))
            qq_ref[p, :, e * TQ:(e + 1) * TQ] = jnp.concatenate(pieces, axis=0)

    def key_chunk(p, j):
        k0 = pl.multiple_of(j * TK, TK)
        return jnp.concatenate([k_ref[0, pl.ds(k0, TK), p * LANES:(p + 1) * LANES],
                                cb_ref[0, pl.ds(k0, TK), :]], axis=1)

    def scores(p, j):
        s = _dot(key_chunk(p, j), qq_ref[p])
        s_ref[p] = s
        return jnp.max(s, axis=0, keepdims=True)

    def absorb(p, j, smax, m_old):
        m = jnp.maximum(m_old, smax)
        alpha = jnp.exp2(m_old - m)
        pb = jnp.exp2(s_ref[p] - m).astype(BF16)
        for e in range(2):
            hd = 2 * p + e
            pv = _dot(vT_ref[0, j, hd * V_ROWS:(hd + 1) * V_ROWS, :], pb[:, e * TQ:(e + 1) * TQ])
            acc_ref[hd] = alpha[:, e * TQ:(e + 1) * TQ] * acc_ref[hd] + pv
        return m

    H = TQ // 2
    future = lax.broadcasted_iota(jnp.int32, (H, H), 0) > lax.broadcasted_iota(jnp.int32, (H, H), 1)

    def late(x):
        return jnp.concatenate([x[:, H:TQ], x[:, TQ + H:]], axis=1)

    def scores_diag(p):
        kk = key_chunk(p, i)
        qq = qq_ref[p]
        s_early = _dot(kk[:H], qq)
        s_late = _dot(kk[H:], late(qq))
        tops_max = []
        for e in range(2):
            sq = jnp.where(future, MASKED, s_early[:, e * TQ:e * TQ + H])
            s_ref[p, 0:H, e * TQ:e * TQ + H] = sq
            s_ref[p, 0:H, e * TQ + H:(e + 1) * TQ] = s_early[:, e * TQ + H:(e + 1) * TQ]
            lq = jnp.where(future, MASKED, s_late[:, e * H:(e + 1) * H])
            s_ref[p, H:2 * H, e * H:(e + 1) * H] = lq
            tops_max += [jnp.max(sq, axis=0, keepdims=True),
                         jnp.maximum(jnp.max(s_early[:, e * TQ + H:(e + 1) * TQ], axis=0, keepdims=True),
                                     jnp.max(lq, axis=0, keepdims=True))]
        return jnp.concatenate(tops_max, axis=1)

    def absorb_diag(p, m):
        p_early = jnp.exp2(s_ref[p, 0:H, :] - m).astype(BF16)
        p_late = jnp.exp2(s_ref[p, H:2 * H, 0:TQ] - late(m)).astype(BF16)
        for e in range(2):
            hd = 2 * p + e
            vt = vT_ref[0, i, hd * V_ROWS:(hd + 1) * V_ROWS, :]
            pv_early = _dot(vt[:, :H], p_early[:, e * TQ:(e + 1) * TQ])
            pv_late = _dot(vt[:, H:], p_late[:, e * H:(e + 1) * H])
            acc_ref[hd] = jnp.concatenate([pv_early[:, :H], pv_early[:, H:] + pv_late], axis=1)
        return m

    def chunk(j, carry):
        smax, ms = carry
        new_m = []
        for p in range(PAIRS):
            nxt_smax = scores(p + 1, j) if p + 1 < PAIRS else scores(0, j + 1)
            new_m.append(absorb(p, j, smax, ms[p]))
            smax = nxt_smax
        return smax, tuple(new_m)

    ms = []
    smax = scores_diag(0)
    for p in range(PAIRS):
        nxt_smax = scores_diag(p + 1) if p + 1 < PAIRS else scores(0, 0)
        ms.append(absorb_diag(p, smax))
        smax = nxt_smax
    carry = (smax, tuple(ms))
    lax.fori_loop(0, i, chunk, carry)

    outs = []
    for hd in range(FOX_HEADS):
        acc = acc_ref[hd]
        outs.append(acc[:FOX_DH] * (1.0 / acc[FOX_DH:FOX_DH + 1]))
    o_ref[0] = jnp.concatenate(outs, axis=0).T


def _fox(qT, cT, k, cb, vT):
    batch, _, seq = qT.shape
    return pl.pallas_call(
        _fox_kernel,
        grid=(batch, seq // TQ),
        in_specs=[pl.BlockSpec((1, FOX_W, TQ), lambda b, i: (b, 0, i)),
                  pl.BlockSpec((1, FOX_HEADS, TQ), lambda b, i: (b, 0, i)),
                  pl.BlockSpec((1, seq, FOX_W), lambda b, i: (b, 0, 0)),
                  pl.BlockSpec((1, seq, LANES), lambda b, i: (b, 0, 0)),
                  pl.BlockSpec((1, seq // TK, FOX_HEADS * V_ROWS, TK), lambda b, i: (b, 0, 0, 0))],
        out_specs=pl.BlockSpec((1, TQ, FOX_W), lambda b, i: (b, i, 0)),
        out_shape=jax.ShapeDtypeStruct((batch, seq, FOX_W), F32),
        scratch_shapes=[pltpu.VMEM((PAIRS, 2 * LANES, 2 * TQ), BF16),
                        pltpu.VMEM((PAIRS, TK, 2 * TQ), F32),
                        pltpu.VMEM((FOX_HEADS, V_ROWS, TQ), F32)],
        compiler_params=_params(2),
        name="fox",
    )(qT, cT, k, cb, vT)


def _memkv_kernel(mem_ref, g_mem_ref, w_ckT_ref, w_cv_ref, g_ck_ref, kcT_ref, vc_ref):
    mn = (_rms(mem_ref[0]) * g_mem_ref[...]).astype(BF16)
    kT = _dot_nt(w_ckT_ref[...], mn)
    dh = kT.shape[0] // CA_HEADS
    for hd in range(CA_HEADS):
        blk = kT[hd * dh:(hd + 1) * dh, :]
        inv = lax.rsqrt(jnp.mean(blk * blk, axis=0, keepdims=True) + EPS)
        kcT_ref[0, hd * dh:(hd + 1) * dh, :] = (blk * inv * g_ck_ref[...]).astype(BF16)
    vc_ref[0] = _dot(mn, w_cv_ref[...]).astype(BF16)


def _memkv(mem, g_mem, w_ckT, w_cv, g_ck_col):
    batch, m, d = mem.shape
    return pl.pallas_call(
        _memkv_kernel,
        grid=(batch,),
        in_specs=[pl.BlockSpec((1, m, d), lambda b: (b, 0, 0)), _resident((1, d)), _resident((d, d)),
                  _resident((d, d)), _resident((d // CA_HEADS, m))],
        out_specs=[pl.BlockSpec((1, d, m), lambda b: (b, 0, 0)), pl.BlockSpec((1, m, d), lambda b: (b, 0, 0))],
        out_shape=[jax.ShapeDtypeStruct((batch, d, m), BF16), jax.ShapeDtypeStruct((batch, m, d), BF16)],
        compiler_params=_params(1),
        name="memkv",
    )(mem, g_mem, w_ckT, w_cv, g_ck_col)


def _tail_kernel(x_ref, attn_ref, yg_ref, g_fo_ref, w_out_ref, g_ca_ref, w_cq_ref, g_cq_ref, kcT_ref, vc_ref,
                 w_co_ref, g_f2_ref, wgu_ref, wo_ref, o_ref, act_ref):
    a = (_rms(attn_ref[...]) * g_fo_ref[...]).astype(BF16)
    x2 = x_ref[...] + _dot(jnp.concatenate([a, yg_ref[...]], axis=1), w_out_ref[...])

    hq = (_rms(x2) * g_ca_ref[...]).astype(BF16)
    qc = _dot(hq, w_cq_ref[...])
    dh = qc.shape[1] // CA_HEADS
    outs = []
    for hd in range(CA_HEADS):
        qh = (_rms(qc[:, hd * dh:(hd + 1) * dh]) * g_cq_ref[...]).astype(BF16)
        s = _dot(qh, kcT_ref[0, hd * dh:(hd + 1) * dh, :])
        pexp = jnp.exp2(s - jnp.max(s, axis=-1, keepdims=True))
        denom = jnp.sum(pexp, axis=-1, keepdims=True)
        o = _dot(pexp.astype(BF16), vc_ref[0, :, hd * dh:(hd + 1) * dh]) * (1.0 / denom)
        outs.append(o.astype(BF16))
    x3 = x2 + _dot(jnp.concatenate(outs, axis=1), w_co_ref[...])

    o_ref[...] = _swiglu_half_step(x3, g_f2_ref, wgu_ref, wo_ref, act_ref)


def _tail(x1, attn, yg, seq, g_fo, w_out, g_ca, w_cq, g_cq, kcT, vc, w_co, g_f2, wgu, wo):
    t, d = x1.shape
    d_ff = wo.shape[0]
    m = vc.shape[1]
    bps = seq // TM

    def rows(width):
        return pl.BlockSpec((TM, width), lambda r: (r, 0))

    return pl.pallas_call(
        _tail_kernel,
        grid=(t // TM,),
        in_specs=[rows(d), rows(FOX_W), rows(GMLP_W), _resident((1, FOX_W)), _resident((d, d)),
                  _resident((1, d)), _resident((d, d)), _resident((1, d // CA_HEADS)),
                  pl.BlockSpec((1, d, m), lambda r: (r // bps, 0, 0)),
                  pl.BlockSpec((1, m, d), lambda r: (r // bps, 0, 0)),
                  _resident((d, d)), _resident((1, d)), _resident((d, 2 * d_ff)), _resident((d_ff, d))],
        out_specs=rows(d),
        out_shape=jax.ShapeDtypeStruct((t, d), F32),
        scratch_shapes=[pltpu.VMEM((TM, d_ff), BF16)],
        compiler_params=_params(1),
        name="tail",
    )(x1, attn, yg, g_fo, w_out, g_ca, w_cq, g_cq, kcT, vc, w_co, g_f2, wgu, wo)


def _placement():
    pm = np.zeros((PAIRS * BIAS_ROWS, 32), np.float32)
    for p in range(PAIRS):
        for e in range(2):
            for part in range(3):
                pm[BIAS_ROWS * p + 3 * e + part, 8 * part + 2 * p + e] = -1.0
        pm[BIAS_ROWS * p + 6:BIAS_ROWS * p + 9, 24] = 1.0
    return pm


def _layer(layer, x, mem, g_ffn1, w_ffn1_in, w_ffn1_out, g_mix, w_in, b_f, g_q, g_k, g_sgu, w_s, b_s, g_fox_o,
           g_gmlp_o, w_out, g_ca, g_mem, w_cq, w_ckv, g_cq, g_ck, w_co, g_ffn2, w_ffn2_in, w_ffn2_out):
    (g_ffn1, g_mix, b_f, g_q, g_k, g_sgu, w_s, b_s, g_fox_o, g_gmlp_o, g_ca, g_mem, w_ckv, g_cq, g_ck,
     g_ffn2) = (v[layer] for v in (g_ffn1, g_mix, b_f, g_q, g_k, g_sgu, w_s, b_s, g_fox_o, g_gmlp_o, g_ca,
                                   g_mem, w_ckv, g_cq, g_ck, g_ffn2))
    batch, seq, d = x.shape
    t = batch * seq
    m = mem.shape[1]
    dh_ca = d // CA_HEADS

    def row(v):
        return v.reshape(1, -1).astype(F32)

    gq = jnp.broadcast_to((g_q * (FOX_DH ** -0.5 * LOG2E))[:, None], (FOX_DH, TM)).astype(F32)
    gk = row(jnp.tile(g_k, FOX_HEADS))
    bfc = jnp.broadcast_to(b_f[:, None], (FOX_HEADS, TM)).astype(F32)
    headsum = jnp.asarray(np.kron(np.eye(FOX_HEADS, dtype=np.float32),
                                  np.ones((FOX_DH, FOX_DH), np.float32)), BF16)
    triu = jnp.asarray(np.triu(np.ones((TM, TM), np.float32)), BF16)
    place = jnp.asarray(_placement(), BF16)
    bs = jnp.repeat(b_s.T, GMLP_DG, axis=1).astype(F32)
    g_cq_s = row(g_cq * (dh_ca ** -0.5 * LOG2E))
    g_ck_col = jnp.broadcast_to(g_ck[:, None], (dh_ca, m)).astype(F32)

    kcT, vc = _memkv(mem, row(g_mem), w_ckv[:, :d].T.astype(BF16), w_ckv[:, d:].astype(BF16), g_ck_col)

    x1, (w2_in, w2_out, w_out_b, w_cq_b, w_co_b), w_nn, w_nt = _ffn(
        x.reshape(t, d), row(g_ffn1), w_ffn1_in, w_ffn1_out, layer, (w_ffn2_in, w_ffn2_out, w_out, w_cq, w_co),
        w_in)
    qT, cT, k, cb, vT, yg = _proj(x1, batch, seq, row(g_mix), w_nn, w_nt, gq, gk, bfc, headsum, triu, place,
                                  row(g_sgu), w_s.astype(BF16), bs, row(g_gmlp_o))
    attn = _fox(qT, cT, k.reshape(batch, seq, FOX_W), cb.reshape(batch, seq, LANES), vT)
    out = _tail(x1, attn.reshape(t, FOX_W), yg, seq, row(g_fox_o), w_out_b, row(g_ca),
                w_cq_b, g_cq_s, kcT, vc, w_co_b, row(g_ffn2), w2_in, w2_out)
    return out.reshape(batch, seq, d)


def kernel(x, mem, g_ffn1, w_ffn1_in, w_ffn1_out, g_mix, w_in, b_f, g_q, g_k, g_sgu, w_s, b_s, g_fox_o,
           g_gmlp_o, w_out, g_ca, g_mem, w_cq, w_ckv, g_cq, g_ck, w_co, g_ffn2, w_ffn2_in, w_ffn2_out):
    layer_params = (g_ffn1, w_ffn1_in, w_ffn1_out, g_mix, w_in, b_f, g_q, g_k, g_sgu, w_s, b_s, g_fox_o,
                    g_gmlp_o, w_out, g_ca, g_mem, w_cq, w_ckv, g_cq, g_ck, w_co, g_ffn2, w_ffn2_in, w_ffn2_out)
    for layer in range(g_ffn1.shape[0]):
        x = _layer(layer, x, mem, *layer_params)
    return x
```

```python
import functools

import jax
import jax.numpy as jnp
import numpy as np
from jax import lax
from jax.experimental import pallas as pl
from jax.experimental.pallas import tpu as pltpu

F32 = jnp.float32
BF16 = jnp.bfloat16
EPS = 1e-6
LOG2E = 1.4426950408889634
MASKED = -1e30

FOX_HEADS = 8
FOX_DH = 64
FOX_W = FOX_HEADS * FOX_DH
GMLP_GROUPS = 8
GMLP_DG = 64
GMLP_W = GMLP_GROUPS * GMLP_DG
CHUNK = 128
CA_HEADS = 4

LANES = 128
VMEM_LIMIT_BYTES = 56 * 1024 * 1024

TM = 512
TQ = 512
TK = 512
FC = 256
PAIRS = FOX_HEADS // 2
BIAS_ROWS = 32
V_ROWS = FOX_DH + 16


def _rms(x):
    return x * lax.rsqrt(jnp.mean(x * x, axis=-1, keepdims=True) + EPS)


def _dot(a, b):
    return lax.dot_general(a, b, (((1,), (0,)), ((), ())), preferred_element_type=F32)


def _dot_nt(a, b):
    return lax.dot_general(a, b, (((1,), (1,)), ((), ())), preferred_element_type=F32)


def _split3(x):
    hi = x.astype(BF16).astype(F32)
    mid = (x - hi).astype(BF16).astype(F32)
    lo = (x - hi - mid).astype(BF16).astype(F32)
    return hi, mid, lo


def _resident(shape):
    return pl.BlockSpec(shape, lambda *_: (0,) * len(shape), pipeline_mode=pl.Buffered(1))


def _params(n_axes):
    return pltpu.CompilerParams(dimension_semantics=("arbitrary",) * n_axes,
                                vmem_limit_bytes=VMEM_LIMIT_BYTES)


def _swiglu_half_step(x, g_ref, wgu_ref, wo_ref, act_ref):
    d_ff = wo_ref.shape[0]
    h = (_rms(x) * g_ref[...]).astype(BF16)
    for c in range(d_ff // FC):
        gate = _dot(h, wgu_ref[:, c * FC:(c + 1) * FC])
        up = _dot(h, wgu_ref[:, d_ff + c * FC:d_ff + (c + 1) * FC])
        act_ref[:, c * FC:(c + 1) * FC] = (gate * jax.nn.sigmoid(gate) * up).astype(BF16)
    return x + 0.5 * _dot(act_ref[...], wo_ref[...])


def _ffn_kernel(x_ref, g_ref, wgu_ref, wo_ref, *refs, n_cast):
    o_ref, act_ref = refs[n_cast], refs[-1]
    for src_ref, dst_ref in zip(refs[:n_cast], refs[n_cast + 1:-1]):
        dst_ref[...] = src_ref[0].astype(BF16)
    o_ref[...] = _swiglu_half_step(x_ref[...], g_ref, wgu_ref.at[0], wo_ref.at[0], act_ref)


def _cast_rows(rows, steps):
    return next(rb for rb in range(16, rows + 1, 16) if rows % rb == 0 and rows // rb <= steps)


def _ffn(x, g, wgu, wo, layer, to_bf16=()):
    t, d = x.shape
    d_ff = wo.shape[1]
    steps = t // TM
    row = pl.BlockSpec((TM, d), lambda r: (r, 0))

    def stacked(shape):
        return pl.BlockSpec((1,) + shape, lambda r: (layer, 0, 0), pipeline_mode=pl.Buffered(1))

    srcs, dsts = [], []
    for w in to_bf16:
        rb = _cast_rows(w.shape[1], steps)
        last = w.shape[1] // rb - 1
        srcs.append(pl.BlockSpec((1, rb, w.shape[2]), lambda r, last=last: (layer, jnp.minimum(r, last), 0)))
        dsts.append(pl.BlockSpec((rb, w.shape[2]), lambda r, last=last: (jnp.minimum(r, last), 0)))
    outs = pl.pallas_call(
        functools.partial(_ffn_kernel, n_cast=len(to_bf16)),
        grid=(steps,),
        in_specs=[row, _resident((1, d)), stacked((d, 2 * d_ff)), stacked((d_ff, d))] + srcs,
        out_specs=[row] + dsts,
        out_shape=[jax.ShapeDtypeStruct((t, d), F32)] + [jax.ShapeDtypeStruct(w.shape[1:], BF16) for w in to_bf16],
        scratch_shapes=[pltpu.VMEM((TM, d_ff), BF16)],
        compiler_params=_params(1),
        name="ffn",
    )(x, g, wgu, wo, *to_bf16)
    return outs[0], outs[1:]


def _gelu_tanh(x):
    c = float(np.sqrt(2.0 / np.pi))
    return x * (0.5 * jnp.tanh(x * (c + (c * 0.044715) * (x * x))) + 0.5)


def _proj_kernel(x_ref, g_mix_ref, w_nn_ref, w_nt_ref, gq_ref, gk_ref, bf_ref, headsum_ref, triu_ref,
                 place_ref, g_sgu_ref, ws_ref, bs_ref, g_go_ref,
                 qT_ref, cT_ref, k_ref, cb_ref, vT_ref, yg_ref,
                 carry_ref, sgu_ref, *, blocks_per_seq):
    @pl.when(pl.program_id(0) % blocks_per_seq == 0)
    def _():
        carry_ref[...] = jnp.zeros_like(carry_ref)

    h = (_rms(x_ref[...]) * g_mix_ref[...]).astype(BF16)

    zT = _dot_nt(w_nt_ref[...], h)
    for hd in range(FOX_HEADS):
        blk = zT[hd * FOX_DH:(hd + 1) * FOX_DH, :]
        inv = lax.rsqrt(jnp.mean(blk * blk, axis=0, keepdims=True) + EPS)
        qT_ref[0, hd * FOX_DH:(hd + 1) * FOX_DH, :] = (blk * inv * gq_ref[...]).astype(BF16)
    ones_rows = jnp.where(lax.broadcasted_iota(jnp.int32, (V_ROWS - FOX_DH, TK), 0) == 0, 1.0, 0.0).astype(BF16)
    for c in range(TM // TK):
        for hd in range(FOX_HEADS):
            vT_ref[0, c, hd * V_ROWS:hd * V_ROWS + FOX_DH, :] = (
                zT[FOX_W + hd * FOX_DH:FOX_W + (hd + 1) * FOX_DH, c * TK:(c + 1) * TK].astype(BF16))
            vT_ref[0, c, hd * V_ROWS + FOX_DH:(hd + 1) * V_ROWS, :] = ones_rows

    f = zT[2 * FOX_W:2 * FOX_W + FOX_HEADS, :] + bf_ref[...]
    logf = (jnp.minimum(f, 0.0) - jnp.log1p(jnp.exp(-jnp.abs(f)))) * LOG2E
    zeros8 = jnp.zeros((8, TM), F32)
    parts = jnp.concatenate(_split3(logf) + (zeros8,), axis=0).astype(BF16)
    cs = _dot(parts, triu_ref[...])
    c = cs[0:8] + cs[8:16] + cs[16:24] + carry_ref[:, 0:1]
    carry_ref[...] = jnp.broadcast_to(c[:, TM - 1:TM], carry_ref.shape)
    cT_ref[0] = c

    one_row = jnp.where(lax.broadcasted_iota(jnp.int32, (8, TM), 0) == 0, 1.0, 0.0)
    cparts = jnp.concatenate(_split3(c) + (one_row,), axis=0).astype(BF16)
    cb_ref[...] = _dot(place_ref[...], cparts).T.astype(BF16)

    yg_ref[...] = _gmlp_branch(h, w_nn_ref, g_sgu_ref, ws_ref, bs_ref, g_go_ref, sgu_ref)

    k = _dot(h, w_nn_ref[:, :FOX_W])
    ksum = _dot((k * k).astype(BF16), headsum_ref[...])
    k_ref[...] = (k * lax.rsqrt(ksum * (1.0 / FOX_DH) + EPS) * gk_ref[...]).astype(BF16)


def _gmlp_branch(h, w_nn_ref, g_sgu_ref, ws_ref, bs_ref, g_go_ref, sgu_ref):
    vg = _gelu_tanh(_dot(h, w_nn_ref[:, FOX_W + GMLP_W:]))
    vgn = (_rms(vg) * g_sgu_ref[...]).astype(BF16)
    u = _gelu_tanh(_dot(h, w_nn_ref[:, FOX_W:FOX_W + GMLP_W]))
    lane = lax.broadcasted_iota(jnp.int32, (CHUNK, LANES), 1)
    tril = (lax.broadcasted_iota(jnp.int32, (CHUNK, CHUNK), 0)
            >= lax.broadcasted_iota(jnp.int32, (CHUNK, CHUNK), 1))
    n_chunks = TM // CHUNK
    for p in range(GMLP_GROUPS // 2):
        lo_blocks, hi_blocks = [], []
        for ch in range(n_chunks):
            vp = vgn[ch * CHUNK:(ch + 1) * CHUNK, p * LANES:(p + 1) * LANES]
            lo_blocks.append(jnp.where(lane < GMLP_DG, vp, jnp.zeros_like(vp)))
            hi_blocks.append(jnp.where(lane >= GMLP_DG, vp, jnp.zeros_like(vp)))
        vcat = jnp.concatenate([jnp.concatenate(lo_blocks, axis=1),
                                jnp.concatenate(hi_blocks, axis=1)], axis=0)
        w0 = ws_ref[2 * p]
        w1 = ws_ref[2 * p + 1]
        wpair = jnp.concatenate([jnp.where(tril, w0, jnp.zeros_like(w0)),
                                 jnp.where(tril, w1, jnp.zeros_like(w1))], axis=1)
        mixed = _dot(wpair, vcat)
        bias = bs_ref[:, p * LANES:(p + 1) * LANES]
        for ch in range(n_chunks):
            sgu_ref[ch * CHUNK:(ch + 1) * CHUNK, p * LANES:(p + 1) * LANES] = (
                u[ch * CHUNK:(ch + 1) * CHUNK, p * LANES:(p + 1) * LANES]
                * (mixed[:, ch * LANES:(ch + 1) * LANES] + bias))
    return (_rms(sgu_ref[...]) * g_go_ref[...]).astype(BF16)


def _proj(x, batch, seq, g_mix, w_nn, w_nt, gq, gk, bfc, headsum, triu, place, g_sgu, ws, bs, g_go):
    t, d = x.shape
    bps = seq // TM
    n_nt = w_nt.shape[0]

    def rows(width):
        return pl.BlockSpec((TM, width), lambda r: (r, 0))

    kern = functools.partial(_proj_kernel, blocks_per_seq=bps)
    return pl.pallas_call(
        kern,
        grid=(t // TM,),
        in_specs=[rows(d), _resident((1, d)), _resident((d, w_nn.shape[1])), _resident((n_nt, d)),
                  _resident((FOX_DH, TM)), _resident((1, FOX_W)), _resident((FOX_HEADS, TM)),
                  _resident((FOX_W, FOX_W)), _resident((TM, TM)), _resident((PAIRS * BIAS_ROWS, 32)),
                  _resident((1, GMLP_W)), _resident((GMLP_GROUPS, CHUNK, CHUNK)),
                  _resident((CHUNK, GMLP_W)), _resident((1, GMLP_W))],
        out_specs=[pl.BlockSpec((1, FOX_W, TM), lambda r: (r // bps, 0, r % bps)),
                   pl.BlockSpec((1, FOX_HEADS, TM), lambda r: (r // bps, 0, r % bps)),
                   rows(FOX_W), rows(LANES),
                   pl.BlockSpec((1, TM // TK, FOX_HEADS * V_ROWS, TK), lambda r: (r // bps, r % bps, 0, 0)),
                   rows(GMLP_W)],
        out_shape=[jax.ShapeDtypeStruct((batch, FOX_W, seq), BF16),
                   jax.ShapeDtypeStruct((batch, FOX_HEADS, seq), F32),
                   jax.ShapeDtypeStruct((t, FOX_W), BF16),
                   jax.ShapeDtypeStruct((t, LANES), BF16),
                   jax.ShapeDtypeStruct((batch, seq // TK, FOX_HEADS * V_ROWS, TK), BF16),
                   jax.ShapeDtypeStruct((t, GMLP_W), BF16)],
        scratch_shapes=[pltpu.VMEM((FOX_HEADS, LANES), F32), pltpu.VMEM((TM, GMLP_W), F32)],
        compiler_params=_params(1),
        name="proj",
    )(x, g_mix, w_nn, w_nt, gq, gk, bfc, headsum, triu, place, g_sgu, ws, bs, g_go)


def _fox_kernel(qT_ref, cT_ref, k_ref, cb_ref, vT_ref, o_ref, qq_ref, s_ref, acc_ref):
    i = pl.program_id(1)

    row = lax.broadcasted_iota(jnp.int32, (BIAS_ROWS, TQ), 0)
    zeros_q = jnp.zeros((FOX_DH, TQ), BF16)
    for p in range(PAIRS):
        for e in range(2):
            hd = 2 * p + e
            qh = qT_ref[0, hd * FOX_DH:(hd + 1) * FOX_DH, :]
            hi, mid, lo = _split3(cT_ref[0, hd:hd + 1, :])
            blk = jnp.where((row >= 3 * e) & (row < 3 * e + 3), 1.0, 0.0)
            blk = jnp.where(row == 6, hi, blk)
            blk = jnp.where(row == 7, mid, blk)
            blk = jnp.where(row == 8, lo, blk)
            pieces = [qh, zeros_q] if e == 0 else [zeros_q, qh]
            if p > 0:
                pieces.append(jnp.zeros((BIAS_ROWS * p, TQ), BF16))
            pieces.append(blk.astype(BF16))
            if p + 1 < PAIRS:
                pieces.append(jnp.zeros((BIAS_ROWS * (PAIRS - 1 - p), TQ), BF16))
            qq_ref[p, :, e * TQ:(e + 1) * TQ] = jnp.concatenate(pieces, axis=0)

    def key_chunk(p, j):
        k0 = pl.multiple_of(j * TK, TK)
        return jnp.concatenate([k_ref[0, pl.ds(k0, TK), p * LANES:(p + 1) * LANES],
                                cb_ref[0, pl.ds(k0, TK), :]], axis=1)

    def scores(p, j):
        s = _dot(key_chunk(p, j), qq_ref[p])
        s_ref[p] = s
        return jnp.max(s, axis=0, keepdims=True)

    def absorb(p, j, smax, m_old):
        m = jnp.maximum(m_old, smax)
        alpha = jnp.exp2(m_old - m)
        pb = jnp.exp2(s_ref[p] - m).astype(BF16)
        for e in range(2):
            hd = 2 * p + e
            pv = _dot(vT_ref[0, j, hd * V_ROWS:(hd + 1) * V_ROWS, :], pb[:, e * TQ:(e + 1) * TQ])
            acc_ref[hd] = alpha[:, e * TQ:(e + 1) * TQ] * acc_ref[hd] + pv
        return m

    H = TQ // 2
    future = lax.broadcasted_iota(jnp.int32, (H, H), 0) > lax.broadcasted_iota(jnp.int32, (H, H), 1)

    def late(x):
        return jnp.concatenate([x[:, H:TQ], x[:, TQ + H:]], axis=1)

    def scores_diag(p):
        kk = key_chunk(p, i)
        qq = qq_ref[p]
        s_early = _dot(kk[:H], qq)
        s_late = _dot(kk[H:], late(qq))
        tops_max = []
        for e in range(2):
            sq = jnp.where(future, MASKED, s_early[:, e * TQ:e * TQ + H])
            s_ref[p, 0:H, e * TQ:e * TQ + H] = sq
            s_ref[p, 0:H, e * TQ + H:(e + 1) * TQ] = s_early[:, e * TQ + H:(e + 1) * TQ]
            lq = jnp.where(future, MASKED, s_late[:, e * H:(e + 1) * H])
            s_ref[p, H:2 * H, e * H:(e + 1) * H] = lq
            tops_max += [jnp.max(sq, axis=0, keepdims=True),
                         jnp.maximum(jnp.max(s_early[:, e * TQ + H:(e + 1) * TQ], axis=0, keepdims=True),
                                     jnp.max(lq, axis=0, keepdims=True))]
        return jnp.concatenate(tops_max, axis=1)

    def absorb_diag(p, m):
        p_early = jnp.exp2(s_ref[p, 0:H, :] - m).astype(BF16)
        p_late = jnp.exp2(s_ref[p, H:2 * H, 0:TQ] - late(m)).astype(BF16)
        for e in range(2):
            hd = 2 * p + e
            vt = vT_ref[0, i, hd * V_ROWS:(hd + 1) * V_ROWS, :]
            pv_early = _dot(vt[:, :H], p_early[:, e * TQ:(e + 1) * TQ])
            pv_late = _dot(vt[:, H:], p_late[:, e * H:(e + 1) * H])
            acc_ref[hd] = jnp.concatenate([pv_early[:, :H], pv_early[:, H:] + pv_late], axis=1)
        return m

    def chunk(j, carry):
        smax, ms = carry
        new_m = []
        for p in range(PAIRS):
            nxt_smax = scores(p + 1, j) if p + 1 < PAIRS else scores(0, j + 1)
            new_m.append(absorb(p, j, smax, ms[p]))
            smax = nxt_smax
        return smax, tuple(new_m)

    ms = []
    smax = scores_diag(0)
    for p in range(PAIRS):
        nxt_smax = scores_diag(p + 1) if p + 1 < PAIRS else scores(0, 0)
        ms.append(absorb_diag(p, smax))
        smax = nxt_smax
    carry = (smax, tuple(ms))
    lax.fori_loop(0, i, chunk, carry)

    outs = []
    for hd in range(FOX_HEADS):
        acc = acc_ref[hd]
        outs.append(acc[:FOX_DH] * (1.0 / acc[FOX_DH:FOX_DH + 1]))
    o_ref[0] = jnp.concatenate(outs, axis=0).T


def _fox(qT, cT, k, cb, vT):
    batch, _, seq = qT.shape
    return pl.pallas_call(
        _fox_kernel,
        grid=(batch, seq // TQ),
        in_specs=[pl.BlockSpec((1, FOX_W, TQ), lambda b, i: (b, 0, i)),
                  pl.BlockSpec((1, FOX_HEADS, TQ), lambda b, i: (b, 0, i)),
                  pl.BlockSpec((1, seq, FOX_W), lambda b, i: (b, 0, 0)),
                  pl.BlockSpec((1, seq, LANES), lambda b, i: (b, 0, 0)),
                  pl.BlockSpec((1, seq // TK, FOX_HEADS * V_ROWS, TK), lambda b, i: (b, 0, 0, 0))],
        out_specs=pl.BlockSpec((1, TQ, FOX_W), lambda b, i: (b, i, 0)),
        out_shape=jax.ShapeDtypeStruct((batch, seq, FOX_W), F32),
        scratch_shapes=[pltpu.VMEM((PAIRS, 2 * LANES, 2 * TQ), BF16),
                        pltpu.VMEM((PAIRS, TK, 2 * TQ), F32),
                        pltpu.VMEM((FOX_HEADS, V_ROWS, TQ), F32)],
        compiler_params=_params(2),
        name="fox",
    )(qT, cT, k, cb, vT)


def _memkv_kernel(mem_ref, g_mem_ref, w_ckv_ref, g_ck_ref, kcT_ref, vc_ref):
    mn = (_rms(mem_ref[0]) * g_mem_ref[...]).astype(BF16)
    d = mn.shape[1]
    dh = d // CA_HEADS
    k = _dot(mn, w_ckv_ref[0, :, :d])
    for hd in range(CA_HEADS):
        kh = _rms(k[:, hd * dh:(hd + 1) * dh]) * g_ck_ref[...]
        kcT_ref[0, hd * dh:(hd + 1) * dh, :] = kh.T.astype(BF16)
    vc_ref[0] = _dot(mn, w_ckv_ref[0, :, d:]).astype(BF16)


def _memkv(mem, g_mem, w_ckv, layer, g_ck):
    batch, m, d = mem.shape
    return pl.pallas_call(
        _memkv_kernel,
        grid=(batch,),
        in_specs=[pl.BlockSpec((1, m, d), lambda b: (b, 0, 0)), _resident((1, d)),
                  pl.BlockSpec((1, d, 2 * d), lambda b: (layer, 0, 0), pipeline_mode=pl.Buffered(1)),
                  _resident((1, d // CA_HEADS))],
        out_specs=[pl.BlockSpec((1, d, m), lambda b: (b, 0, 0)), pl.BlockSpec((1, m, d), lambda b: (b, 0, 0))],
        out_shape=[jax.ShapeDtypeStruct((batch, d, m), BF16), jax.ShapeDtypeStruct((batch, m, d), BF16)],
        compiler_params=_params(1),
        name="memkv",
    )(mem, g_mem, w_ckv, g_ck)


def _tail_kernel(x_ref, attn_ref, yg_ref, g_fo_ref, w_out_ref, g_ca_ref, w_cq_ref, g_cq_ref, kcT_ref, vc_ref,
                 w_co_ref, g_f2_ref, wgu_ref, wo_ref, o_ref, act_ref):
    a = (_rms(attn_ref[...]) * g_fo_ref[...]).astype(BF16)
    x2 = x_ref[...] + _dot(jnp.concatenate([a, yg_ref[...]], axis=1), w_out_ref[...])

    hq = (_rms(x2) * g_ca_ref[...]).astype(BF16)
    qc = _dot(hq, w_cq_ref[...])
    dh = qc.shape[1] // CA_HEADS
    heads = [slice(hd * dh, (hd + 1) * dh) for hd in range(CA_HEADS)]
    qh = [(_rms(qc[:, c]) * g_cq_ref[...]).astype(BF16) for c in heads]
    s = [_dot(q, kcT_ref[0, c, :]) for q, c in zip(qh, heads)]
    pexp = [jnp.exp2(v - jnp.max(v, axis=-1, keepdims=True)) for v in s]
    outs = [(_dot(p.astype(BF16), vc_ref[0, :, c]) * (1.0 / jnp.sum(p, axis=-1, keepdims=True))).astype(BF16)
            for p, c in zip(pexp, heads)]
    x3 = x2 + _dot(jnp.concatenate(outs, axis=1), w_co_ref[...])

    o_ref[...] = _swiglu_half_step(x3, g_f2_ref, wgu_ref, wo_ref, act_ref)


def _tail(x1, attn, yg, seq, g_fo, w_out, g_ca, w_cq, g_cq, kcT, vc, w_co, g_f2, wgu, wo):
    t, d = x1.shape
    d_ff = wo.shape[0]
    m = vc.shape[1]
    bps = seq // TM

    def rows(width):
        return pl.BlockSpec((TM, width), lambda r: (r, 0))

    return pl.pallas_call(
        _tail_kernel,
        grid=(t // TM,),
        in_specs=[rows(d), rows(FOX_W), rows(GMLP_W), _resident((1, FOX_W)), _resident((d, d)),
                  _resident((1, d)), _resident((d, d)), _resident((1, d // CA_HEADS)),
                  pl.BlockSpec((1, d, m), lambda r: (r // bps, 0, 0)),
                  pl.BlockSpec((1, m, d), lambda r: (r // bps, 0, 0)),
                  _resident((d, d)), _resident((1, d)), _resident((d, 2 * d_ff)), _resident((d_ff, d))],
        out_specs=rows(d),
        out_shape=jax.ShapeDtypeStruct((t, d), F32),
        scratch_shapes=[pltpu.VMEM((TM, d_ff), BF16)],
        compiler_params=_params(1),
        name="tail",
    )(x1, attn, yg, g_fo, w_out, g_ca, w_cq, g_cq, kcT, vc, w_co, g_f2, wgu, wo)


def _placement():
    pm = np.zeros((PAIRS * BIAS_ROWS, 32), np.float32)
    for p in range(PAIRS):
        for e in range(2):
            for part in range(3):
                pm[BIAS_ROWS * p + 3 * e + part, 8 * part + 2 * p + e] = -1.0
        pm[BIAS_ROWS * p + 6:BIAS_ROWS * p + 9, 24] = 1.0
    return pm


def _layer(layer, x, mem, g_ffn1, w_ffn1_in, w_ffn1_out, g_mix, w_in, b_f, g_q, g_k, g_sgu, w_s, b_s, g_fox_o,
           g_gmlp_o, w_out, g_ca, g_mem, w_cq, w_ckv, g_cq, g_ck, w_co, g_ffn2, w_ffn2_in, w_ffn2_out):
    (g_ffn1, g_mix, w_in, b_f, g_q, g_k, g_sgu, w_s, b_s, g_fox_o, g_gmlp_o, g_ca, g_mem, g_cq, g_ck,
     g_ffn2) = (v[layer] for v in (g_ffn1, g_mix, w_in, b_f, g_q, g_k, g_sgu, w_s, b_s, g_fox_o, g_gmlp_o, g_ca,
                                   g_mem, g_cq, g_ck, g_ffn2))
    batch, seq, d = x.shape
    t = batch * seq
    dh_ca = d // CA_HEADS

    def row(v):
        return v.reshape(1, -1).astype(F32)

    k_off, v_off, f_off, uv_off = FOX_W, 2 * FOX_W, 3 * FOX_W, 3 * FOX_W + FOX_HEADS
    w_nn = jnp.concatenate([w_in[:, k_off:v_off], w_in[:, uv_off:]], axis=1).astype(BF16)
    w_nt = jnp.concatenate([w_in[:, :k_off].T, w_in[:, v_off:f_off].T, w_in[:, f_off:uv_off].T,
                            jnp.zeros((8, d), F32)], axis=0).astype(BF16)
    gq = jnp.broadcast_to((g_q * (FOX_DH ** -0.5 * LOG2E))[:, None], (FOX_DH, TM)).astype(F32)
    gk = row(jnp.tile(g_k, FOX_HEADS))
    bfc = jnp.broadcast_to(b_f[:, None], (FOX_HEADS, TM)).astype(F32)
    headsum = jnp.asarray(np.kron(np.eye(FOX_HEADS, dtype=np.float32),
                                  np.ones((FOX_DH, FOX_DH), np.float32)), BF16)
    triu = jnp.asarray(np.triu(np.ones((TM, TM), np.float32)), BF16)
    place = jnp.asarray(_placement(), BF16)
    bs = jnp.repeat(b_s.T, GMLP_DG, axis=1).astype(F32)
    g_cq_s = row(g_cq * (dh_ca ** -0.5 * LOG2E))

    kcT, vc = _memkv(mem, row(g_mem), w_ckv, layer, row(g_ck))

    x1, (w2_in, w2_out, w_out_b, w_cq_b, w_co_b) = _ffn(
        x.reshape(t, d), row(g_ffn1), w_ffn1_in, w_ffn1_out, layer, (w_ffn2_in, w_ffn2_out, w_out, w_cq, w_co))
    qT, cT, k, cb, vT, yg = _proj(x1, batch, seq, row(g_mix), w_nn, w_nt, gq, gk, bfc, headsum, triu, place,
                                  row(g_sgu), w_s.astype(BF16), bs, row(g_gmlp_o))
    attn = _fox(qT, cT, k.reshape(batch, seq, FOX_W), cb.reshape(batch, seq, LANES), vT)
    out = _tail(x1, attn.reshape(t, FOX_W), yg, seq, row(g_fox_o), w_out_b, row(g_ca),
                w_cq_b, g_cq_s, kcT, vc, w_co_b, row(g_ffn2), w2_in, w2_out)
    return out.reshape(batch, seq, d)


def kernel(x, mem, g_ffn1, w_ffn1_in, w_ffn1_out, g_mix, w_in, b_f, g_q, g_k, g_sgu, w_s, b_s, g_fox_o,
           g_gmlp_o, w_out, g_ca, g_mem, w_cq, w_ckv, g_cq, g_ck, w_co, g_ffn2, w_ffn2_in, w_ffn2_out):
    layer_params = (g_ffn1, w_ffn1_in, w_ffn1_out, g_mix, w_in, b_f, g_q, g_k, g_sgu, w_s, b_s, g_fox_o,
                    g_gmlp_o, w_out, g_ca, g_mem, w_cq, w_ckv, g_cq, g_ck, w_co, g_ffn2, w_ffn2_in, w_ffn2_out)
    for layer in range(g_ffn1.shape[0]):
        x = _layer(layer, x, mem, *layer_params)
    return x
```

```python
import functools

import jax
import jax.numpy as jnp
import numpy as np
from jax import lax
from jax.experimental import pallas as pl
from jax.experimental.pallas import tpu as pltpu

F32 = jnp.float32
BF16 = jnp.bfloat16
EPS = 1e-6
LOG2E = 1.4426950408889634
MASKED = -1e30

FOX_HEADS = 8
FOX_DH = 64
FOX_W = FOX_HEADS * FOX_DH
GMLP_GROUPS = 8
GMLP_DG = 64
GMLP_W = GMLP_GROUPS * GMLP_DG
CHUNK = 128
CA_HEADS = 4

LANES = 128
VMEM_LIMIT_BYTES = 56 * 1024 * 1024

TM = 512
TQ = 512
TK = 512
FC = 256
PAIRS = FOX_HEADS // 2
BIAS_ROWS = 32
V_ROWS = FOX_DH + 16


def _rms(x):
    return x * lax.rsqrt(jnp.mean(x * x, axis=-1, keepdims=True) + EPS)


def _dot(a, b):
    return lax.dot_general(a, b, (((1,), (0,)), ((), ())), preferred_element_type=F32)


def _dot_nt(a, b):
    return lax.dot_general(a, b, (((1,), (1,)), ((), ())), preferred_element_type=F32)


def _split3(x):
    hi = x.astype(BF16).astype(F32)
    mid = (x - hi).astype(BF16).astype(F32)
    lo = (x - hi - mid).astype(BF16).astype(F32)
    return hi, mid, lo


def _resident(shape):
    return pl.BlockSpec(shape, lambda *_: (0,) * len(shape), pipeline_mode=pl.Buffered(1))


def _params(n_axes):
    return pltpu.CompilerParams(dimension_semantics=("arbitrary",) * n_axes,
                                vmem_limit_bytes=VMEM_LIMIT_BYTES)


def _swiglu_half_step(x, g_ref, wgu_ref, wo_ref, act_ref):
    d_ff = wo_ref.shape[0]
    h = (_rms(x) * g_ref[...]).astype(BF16)
    for c in range(d_ff // FC):
        gate = _dot(h, wgu_ref[:, c * FC:(c + 1) * FC])
        up = _dot(h, wgu_ref[:, d_ff + c * FC:d_ff + (c + 1) * FC])
        act_ref[:, c * FC:(c + 1) * FC] = (gate * jax.nn.sigmoid(gate) * up).astype(BF16)
    return x + 0.5 * _dot(act_ref[...], wo_ref[...])


def _ffn_kernel(x_ref, g_ref, wgu_ref, wo_ref, *refs, n_cast):
    o_ref, act_ref = refs[n_cast], refs[-1]
    for src_ref, dst_ref in zip(refs[:n_cast], refs[n_cast + 1:-1]):
        dst_ref[...] = src_ref[0].astype(BF16)
    o_ref[...] = _swiglu_half_step(x_ref[...], g_ref, wgu_ref.at[0], wo_ref.at[0], act_ref)


def _cast_rows(rows, steps):
    return next(rb for rb in range(16, rows + 1, 16) if rows % rb == 0 and rows // rb <= steps)


def _ffn(x, g, wgu, wo, layer, to_bf16=()):
    t, d = x.shape
    d_ff = wo.shape[1]
    steps = t // TM
    row = pl.BlockSpec((TM, d), lambda r: (r, 0))

    def stacked(shape):
        return pl.BlockSpec((1,) + shape, lambda r: (layer, 0, 0), pipeline_mode=pl.Buffered(1))

    srcs, dsts = [], []
    for w in to_bf16:
        rb = _cast_rows(w.shape[1], steps)
        last = w.shape[1] // rb - 1
        srcs.append(pl.BlockSpec((1, rb, w.shape[2]), lambda r, last=last: (layer, jnp.minimum(r, last), 0)))
        dsts.append(pl.BlockSpec((rb, w.shape[2]), lambda r, last=last: (jnp.minimum(r, last), 0)))
    outs = pl.pallas_call(
        functools.partial(_ffn_kernel, n_cast=len(to_bf16)),
        grid=(steps,),
        in_specs=[row, _resident((1, d)), stacked((d, 2 * d_ff)), stacked((d_ff, d))] + srcs,
        out_specs=[row] + dsts,
        out_shape=[jax.ShapeDtypeStruct((t, d), F32)] + [jax.ShapeDtypeStruct(w.shape[1:], BF16) for w in to_bf16],
        scratch_shapes=[pltpu.VMEM((TM, d_ff), BF16)],
        compiler_params=_params(1),
        name="ffn",
    )(x, g, wgu, wo, *to_bf16)
    return outs[0], outs[1:]


def _gelu_tanh(x):
    c = float(np.sqrt(2.0 / np.pi))
    return x * (0.5 * jnp.tanh(x * (c + (c * 0.044715) * (x * x))) + 0.5)


def _proj_kernel(x_ref, g_mix_ref, w_nn_ref, w_nt_ref, gq_ref, gk_ref, bf_ref, headsum_ref, triu_ref,
                 place_ref, g_sgu_ref, ws_ref, bs_ref, g_go_ref,
                 qT_ref, cT_ref, k_ref, cb_ref, vT_ref, yg_ref,
                 carry_ref, sgu_ref, *, blocks_per_seq):
    @pl.when(pl.program_id(0) % blocks_per_seq == 0)
    def _():
        carry_ref[...] = jnp.zeros_like(carry_ref)

    h = (_rms(x_ref[...]) * g_mix_ref[...]).astype(BF16)

    z_vg = _dot(h, w_nn_ref[:, FOX_W + GMLP_W:])
    z_u = _dot(h, w_nn_ref[:, FOX_W:FOX_W + GMLP_W])
    zT = _dot_nt(w_nt_ref[...], h)
    k = _dot(h, w_nn_ref[:, :FOX_W])

    yg_ref[...] = _gmlp_branch(z_u, z_vg, g_sgu_ref, ws_ref, bs_ref, g_go_ref, sgu_ref)

    for hd in range(FOX_HEADS):
        blk = zT[hd * FOX_DH:(hd + 1) * FOX_DH, :]
        inv = lax.rsqrt(jnp.mean(blk * blk, axis=0, keepdims=True) + EPS)
        qT_ref[0, hd * FOX_DH:(hd + 1) * FOX_DH, :] = (blk * inv * gq_ref[...]).astype(BF16)
    ones_rows = jnp.where(lax.broadcasted_iota(jnp.int32, (V_ROWS - FOX_DH, TK), 0) == 0, 1.0, 0.0).astype(BF16)
    for c in range(TM // TK):
        for hd in range(FOX_HEADS):
            vT_ref[0, c, hd * V_ROWS:hd * V_ROWS + FOX_DH, :] = (
                zT[FOX_W + hd * FOX_DH:FOX_W + (hd + 1) * FOX_DH, c * TK:(c + 1) * TK].astype(BF16))
            vT_ref[0, c, hd * V_ROWS + FOX_DH:(hd + 1) * V_ROWS, :] = ones_rows

    f = zT[2 * FOX_W:2 * FOX_W + FOX_HEADS, :] + bf_ref[...]
    logf = (jnp.minimum(f, 0.0) - jnp.log1p(jnp.exp(-jnp.abs(f)))) * LOG2E
    zeros8 = jnp.zeros((8, TM), F32)
    parts = jnp.concatenate(_split3(logf) + (zeros8,), axis=0).astype(BF16)
    cs = _dot(parts, triu_ref[...])
    c = cs[0:8] + cs[8:16] + cs[16:24] + carry_ref[:, 0:1]
    carry_ref[...] = jnp.broadcast_to(c[:, TM - 1:TM], carry_ref.shape)
    cT_ref[0] = c

    one_row = jnp.where(lax.broadcasted_iota(jnp.int32, (8, TM), 0) == 0, 1.0, 0.0)
    cparts = jnp.concatenate(_split3(c) + (one_row,), axis=0).astype(BF16)
    cb_ref[...] = _dot(place_ref[...], cparts).T.astype(BF16)

    ksum = _dot((k * k).astype(BF16), headsum_ref[...])
    k_ref[...] = (k * lax.rsqrt(ksum * (1.0 / FOX_DH) + EPS) * gk_ref[...]).astype(BF16)


def _gmlp_branch(z_u, z_vg, g_sgu_ref, ws_ref, bs_ref, g_go_ref, sgu_ref):
    vg = _gelu_tanh(z_vg)
    vgn = (_rms(vg) * g_sgu_ref[...]).astype(BF16)
    u = _gelu_tanh(z_u)
    lane = lax.broadcasted_iota(jnp.int32, (CHUNK, LANES), 1)
    tril = (lax.broadcasted_iota(jnp.int32, (CHUNK, CHUNK), 0)
            >= lax.broadcasted_iota(jnp.int32, (CHUNK, CHUNK), 1))
    n_chunks = TM // CHUNK
    for p in range(GMLP_GROUPS // 2):
        lo_blocks, hi_blocks = [], []
        for ch in range(n_chunks):
            vp = vgn[ch * CHUNK:(ch + 1) * CHUNK, p * LANES:(p + 1) * LANES]
            lo_blocks.append(jnp.where(lane < GMLP_DG, vp, jnp.zeros_like(vp)))
            hi_blocks.append(jnp.where(lane >= GMLP_DG, vp, jnp.zeros_like(vp)))
        vcat = jnp.concatenate([jnp.concatenate(lo_blocks, axis=1),
                                jnp.concatenate(hi_blocks, axis=1)], axis=0)
        w0 = ws_ref[2 * p]
        w1 = ws_ref[2 * p + 1]
        wpair = jnp.concatenate([jnp.where(tril, w0, jnp.zeros_like(w0)),
                                 jnp.where(tril, w1, jnp.zeros_like(w1))], axis=1)
        mixed = _dot(wpair, vcat)
        bias = bs_ref[:, p * LANES:(p + 1) * LANES]
        for ch in range(n_chunks):
            sgu_ref[ch * CHUNK:(ch + 1) * CHUNK, p * LANES:(p + 1) * LANES] = (
                u[ch * CHUNK:(ch + 1) * CHUNK, p * LANES:(p + 1) * LANES]
                * (mixed[:, ch * LANES:(ch + 1) * LANES] + bias))
    return (_rms(sgu_ref[...]) * g_go_ref[...]).astype(BF16)


def _proj(x, batch, seq, g_mix, w_nn, w_nt, gq, gk, bfc, headsum, triu, place, g_sgu, ws, bs, g_go):
    t, d = x.shape
    bps = seq // TM
    n_nt = w_nt.shape[0]

    def rows(width):
        return pl.BlockSpec((TM, width), lambda r: (r, 0))

    kern = functools.partial(_proj_kernel, blocks_per_seq=bps)
    return pl.pallas_call(
        kern,
        grid=(t // TM,),
        in_specs=[rows(d), _resident((1, d)), _resident((d, w_nn.shape[1])), _resident((n_nt, d)),
                  _resident((FOX_DH, TM)), _resident((1, FOX_W)), _resident((FOX_HEADS, TM)),
                  _resident((FOX_W, FOX_W)), _resident((TM, TM)), _resident((PAIRS * BIAS_ROWS, 32)),
                  _resident((1, GMLP_W)), _resident((GMLP_GROUPS, CHUNK, CHUNK)),
                  _resident((CHUNK, GMLP_W)), _resident((1, GMLP_W))],
        out_specs=[pl.BlockSpec((1, FOX_W, TM), lambda r: (r // bps, 0, r % bps)),
                   pl.BlockSpec((1, FOX_HEADS, TM), lambda r: (r // bps, 0, r % bps)),
                   rows(FOX_W), rows(LANES),
                   pl.BlockSpec((1, TM // TK, FOX_HEADS * V_ROWS, TK), lambda r: (r // bps, r % bps, 0, 0)),
                   rows(GMLP_W)],
        out_shape=[jax.ShapeDtypeStruct((batch, FOX_W, seq), BF16),
                   jax.ShapeDtypeStruct((batch, FOX_HEADS, seq), F32),
                   jax.ShapeDtypeStruct((t, FOX_W), BF16),
                   jax.ShapeDtypeStruct((t, LANES), BF16),
                   jax.ShapeDtypeStruct((batch, seq // TK, FOX_HEADS * V_ROWS, TK), BF16),
                   jax.ShapeDtypeStruct((t, GMLP_W), BF16)],
        scratch_shapes=[pltpu.VMEM((FOX_HEADS, LANES), F32), pltpu.VMEM((TM, GMLP_W), F32)],
        compiler_params=_params(1),
        name="proj",
    )(x, g_mix, w_nn, w_nt, gq, gk, bfc, headsum, triu, place, g_sgu, ws, bs, g_go)


def _fox_kernel(qT_ref, cT_ref, k_ref, cb_ref, vT_ref, o_ref, qq_ref, s_ref, acc_ref):
    i = pl.program_id(1)

    row = lax.broadcasted_iota(jnp.int32, (BIAS_ROWS, TQ), 0)
    zeros_q = jnp.zeros((FOX_DH, TQ), BF16)
    for p in range(PAIRS):
        for e in range(2):
            hd = 2 * p + e
            qh = qT_ref[0, hd * FOX_DH:(hd + 1) * FOX_DH, :]
            hi, mid, lo = _split3(cT_ref[0, hd:hd + 1, :])
            blk = jnp.where((row >= 3 * e) & (row < 3 * e + 3), 1.0, 0.0)
            blk = jnp.where(row == 6, hi, blk)
            blk = jnp.where(row == 7, mid, blk)
            blk = jnp.where(row == 8, lo, blk)
            pieces = [qh, zeros_q] if e == 0 else [zeros_q, qh]
            if p > 0:
                pieces.append(jnp.zeros((BIAS_ROWS * p, TQ), BF16))
            pieces.append(blk.astype(BF16))
            if p + 1 < PAIRS:
                pieces.append(jnp.zeros((BIAS_ROWS * (PAIRS - 1 - p), TQ), BF16))
            qq_ref[p, :, e * TQ:(e + 1) * TQ] = jnp.concatenate(pieces, axis=0)

    def key_chunk(p, j):
        k0 = pl.multiple_of(j * TK, TK)
        return jnp.concatenate([k_ref[0, pl.ds(k0, TK), p * LANES:(p + 1) * LANES],
                                cb_ref[0, pl.ds(k0, TK), :]], axis=1)

    def scores(p, j):
        s = _dot(key_chunk(p, j), qq_ref[p])
        s_ref[p] = s
        return jnp.max(s, axis=0, keepdims=True)

    def absorb(p, j, smax, m_old):
        m = jnp.maximum(m_old, smax)
        alpha = jnp.exp2(m_old - m)
        pb = jnp.exp2(s_ref[p] - m).astype(BF16)
        for e in range(2):
            hd = 2 * p + e
            pv = _dot(vT_ref[0, j, hd * V_ROWS:(hd + 1) * V_ROWS, :], pb[:, e * TQ:(e + 1) * TQ])
            acc_ref[hd] = alpha[:, e * TQ:(e + 1) * TQ] * acc_ref[hd] + pv
        return m

    H = TQ // 2
    future = lax.broadcasted_iota(jnp.int32, (H, H), 0) > lax.broadcasted_iota(jnp.int32, (H, H), 1)

    def late(x):
        return jnp.concatenate([x[:, H:TQ], x[:, TQ + H:]], axis=1)

    def scores_diag(p):
        kk = key_chunk(p, i)
        qq = qq_ref[p]
        s_early = _dot(kk[:H], qq)
        s_late = _dot(kk[H:], late(qq))
        tops_max = []
        for e in range(2):
            sq = jnp.where(future, MASKED, s_early[:, e * TQ:e * TQ + H])
            s_ref[p, 0:H, e * TQ:e * TQ + H] = sq
            s_ref[p, 0:H, e * TQ + H:(e + 1) * TQ] = s_early[:, e * TQ + H:(e + 1) * TQ]
            lq = jnp.where(future, MASKED, s_late[:, e * H:(e + 1) * H])
            s_ref[p, H:2 * H, e * H:(e + 1) * H] = lq
            tops_max += [jnp.max(sq, axis=0, keepdims=True),
                         jnp.maximum(jnp.max(s_early[:, e * TQ + H:(e + 1) * TQ], axis=0, keepdims=True),
                                     jnp.max(lq, axis=0, keepdims=True))]
        return jnp.concatenate(tops_max, axis=1)

    def absorb_diag(p, m):
        p_early = jnp.exp2(s_ref[p, 0:H, :] - m).astype(BF16)
        p_late = jnp.exp2(s_ref[p, H:2 * H, 0:TQ] - late(m)).astype(BF16)
        for e in range(2):
            hd = 2 * p + e
            vt = vT_ref[0, i, hd * V_ROWS:(hd + 1) * V_ROWS, :]
            pv_early = _dot(vt[:, :H], p_early[:, e * TQ:(e + 1) * TQ])
            pv_late = _dot(vt[:, H:], p_late[:, e * H:(e + 1) * H])
            acc_ref[hd] = jnp.concatenate([pv_early[:, :H], pv_early[:, H:] + pv_late], axis=1)
        return m

    def chunk(j, carry):
        smax, ms = carry
        new_m = []
        for p in range(PAIRS):
            nxt_smax = scores(p + 1, j) if p + 1 < PAIRS else scores(0, j + 1)
            new_m.append(absorb(p, j, smax, ms[p]))
            smax = nxt_smax
        return smax, tuple(new_m)

    ms = []
    smax = scores_diag(0)
    for p in range(PAIRS):
        nxt_smax = scores_diag(p + 1) if p + 1 < PAIRS else scores(0, 0)
        ms.append(absorb_diag(p, smax))
        smax = nxt_smax
    carry = (smax, tuple(ms))
    lax.fori_loop(0, i, chunk, carry)

    outs = []
    for hd in range(FOX_HEADS):
        acc = acc_ref[hd]
        outs.append(acc[:FOX_DH] * (1.0 / acc[FOX_DH:FOX_DH + 1]))
    o_ref[0] = jnp.concatenate(outs, axis=0).T


def _fox(qT, cT, k, cb, vT):
    batch, _, seq = qT.shape
    return pl.pallas_call(
        _fox_kernel,
        grid=(batch, seq // TQ),
        in_specs=[pl.BlockSpec((1, FOX_W, TQ), lambda b, i: (b, 0, i)),
                  pl.BlockSpec((1, FOX_HEADS, TQ), lambda b, i: (b, 0, i)),
                  pl.BlockSpec((1, seq, FOX_W), lambda b, i: (b, 0, 0)),
                  pl.BlockSpec((1, seq, LANES), lambda b, i: (b, 0, 0)),
                  pl.BlockSpec((1, seq // TK, FOX_HEADS * V_ROWS, TK), lambda b, i: (b, 0, 0, 0))],
        out_specs=pl.BlockSpec((1, TQ, FOX_W), lambda b, i: (b, i, 0)),
        out_shape=jax.ShapeDtypeStruct((batch, seq, FOX_W), F32),
        scratch_shapes=[pltpu.VMEM((PAIRS, 2 * LANES, 2 * TQ), BF16),
                        pltpu.VMEM((PAIRS, TK, 2 * TQ), F32),
                        pltpu.VMEM((FOX_HEADS, V_ROWS, TQ), F32)],
        compiler_params=_params(2),
        name="fox",
    )(qT, cT, k, cb, vT)


def _memkv_kernel(mem_ref, g_mem_ref, w_ckv_ref, g_ck_ref, kcT_ref, vc_ref):
    mn = (_rms(mem_ref[0]) * g_mem_ref[...]).astype(BF16)
    d = mn.shape[1]
    dh = d // CA_HEADS
    k = _dot(mn, w_ckv_ref[0, :, :d])
    for hd in range(CA_HEADS):
        kh = _rms(k[:, hd * dh:(hd + 1) * dh]) * g_ck_ref[...]
        kcT_ref[0, hd * dh:(hd + 1) * dh, :] = kh.T.astype(BF16)
    vc_ref[0] = _dot(mn, w_ckv_ref[0, :, d:]).astype(BF16)


def _memkv(mem, g_mem, w_ckv, layer, g_ck):
    batch, m, d = mem.shape
    return pl.pallas_call(
        _memkv_kernel,
        grid=(batch,),
        in_specs=[pl.BlockSpec((1, m, d), lambda b: (b, 0, 0)), _resident((1, d)),
                  pl.BlockSpec((1, d, 2 * d), lambda b: (layer, 0, 0), pipeline_mode=pl.Buffered(1)),
                  _resident((1, d // CA_HEADS))],
        out_specs=[pl.BlockSpec((1, d, m), lambda b: (b, 0, 0)), pl.BlockSpec((1, m, d), lambda b: (b, 0, 0))],
        out_shape=[jax.ShapeDtypeStruct((batch, d, m), BF16), jax.ShapeDtypeStruct((batch, m, d), BF16)],
        compiler_params=_params(1),
        name="memkv",
    )(mem, g_mem, w_ckv, g_ck)


def _tail_kernel(x_ref, attn_ref, yg_ref, g_fo_ref, w_out_ref, g_ca_ref, w_cq_ref, g_cq_ref, kcT_ref, vc_ref,
                 w_co_ref, g_f2_ref, wgu_ref, wo_ref, o_ref, act_ref):
    a = (_rms(attn_ref[...]) * g_fo_ref[...]).astype(BF16)
    x2 = x_ref[...] + _dot(jnp.concatenate([a, yg_ref[...]], axis=1), w_out_ref[...])

    hq = (_rms(x2) * g_ca_ref[...]).astype(BF16)
    qc = _dot(hq, w_cq_ref[...])
    dh = qc.shape[1] // CA_HEADS
    heads = [slice(hd * dh, (hd + 1) * dh) for hd in range(CA_HEADS)]
    qh = [(_rms(qc[:, c]) * g_cq_ref[...]).astype(BF16) for c in heads]
    s = [_dot(q, kcT_ref[0, c, :]) for q, c in zip(qh, heads)]
    pexp = [jnp.exp2(v - jnp.max(v, axis=-1, keepdims=True)) for v in s]
    outs = [(_dot(p.astype(BF16), vc_ref[0, :, c]) * (1.0 / jnp.sum(p, axis=-1, keepdims=True))).astype(BF16)
            for p, c in zip(pexp, heads)]
    x3 = x2 + _dot(jnp.concatenate(outs, axis=1), w_co_ref[...])

    o_ref[...] = _swiglu_half_step(x3, g_f2_ref, wgu_ref, wo_ref, act_ref)


def _tail(x1, attn, yg, seq, g_fo, w_out, g_ca, w_cq, g_cq, kcT, vc, w_co, g_f2, wgu, wo):
    t, d = x1.shape
    d_ff = wo.shape[0]
    m = vc.shape[1]
    bps = seq // TM

    def rows(width):
        return pl.BlockSpec((TM, width), lambda r: (r, 0))

    return pl.pallas_call(
        _tail_kernel,
        grid=(t // TM,),
        in_specs=[rows(d), rows(FOX_W), rows(GMLP_W), _resident((1, FOX_W)), _resident((d, d)),
                  _resident((1, d)), _resident((d, d)), _resident((1, d // CA_HEADS)),
                  pl.BlockSpec((1, d, m), lambda r: (r // bps, 0, 0)),
                  pl.BlockSpec((1, m, d), lambda r: (r // bps, 0, 0)),
                  _resident((d, d)), _resident((1, d)), _resident((d, 2 * d_ff)), _resident((d_ff, d))],
        out_specs=rows(d),
        out_shape=jax.ShapeDtypeStruct((t, d), F32),
        scratch_shapes=[pltpu.VMEM((TM, d_ff), BF16)],
        compiler_params=_params(1),
        name="tail",
    )(x1, attn, yg, g_fo, w_out, g_ca, w_cq, g_cq, kcT, vc, w_co, g_f2, wgu, wo)


def _placement():
    pm = np.zeros((PAIRS * BIAS_ROWS, 32), np.float32)
    for p in range(PAIRS):
        for e in range(2):
            for part in range(3):
                pm[BIAS_ROWS * p + 3 * e + part, 8 * part + 2 * p + e] = -1.0
        pm[BIAS_ROWS * p + 6:BIAS_ROWS * p + 9, 24] = 1.0
    return pm


def _layer(layer, x, mem, g_ffn1, w_ffn1_in, w_ffn1_out, g_mix, w_in, b_f, g_q, g_k, g_sgu, w_s, b_s, g_fox_o,
           g_gmlp_o, w_out, g_ca, g_mem, w_cq, w_ckv, g_cq, g_ck, w_co, g_ffn2, w_ffn2_in, w_ffn2_out):
    (g_ffn1, g_mix, w_in, b_f, g_q, g_k, g_sgu, w_s, b_s, g_fox_o, g_gmlp_o, g_ca, g_mem, g_cq, g_ck,
     g_ffn2) = (v[layer] for v in (g_ffn1, g_mix, w_in, b_f, g_q, g_k, g_sgu, w_s, b_s, g_fox_o, g_gmlp_o, g_ca,
                                   g_mem, g_cq, g_ck, g_ffn2))
    batch, seq, d = x.shape
    t = batch * seq
    dh_ca = d // CA_HEADS

    def row(v):
        return v.reshape(1, -1).astype(F32)

    k_off, v_off, f_off, uv_off = FOX_W, 2 * FOX_W, 3 * FOX_W, 3 * FOX_W + FOX_HEADS
    w_nn = jnp.concatenate([w_in[:, k_off:v_off], w_in[:, uv_off:]], axis=1).astype(BF16)
    w_nt = jnp.concatenate([w_in[:, :k_off].T, w_in[:, v_off:f_off].T, w_in[:, f_off:uv_off].T,
                            jnp.zeros((8, d), F32)], axis=0).astype(BF16)
    gq = jnp.broadcast_to((g_q * (FOX_DH ** -0.5 * LOG2E))[:, None], (FOX_DH, TM)).astype(F32)
    gk = row(jnp.tile(g_k, FOX_HEADS))
    bfc = jnp.broadcast_to(b_f[:, None], (FOX_HEADS, TM)).astype(F32)
    headsum = jnp.asarray(np.kron(np.eye(FOX_HEADS, dtype=np.float32),
                                  np.ones((FOX_DH, FOX_DH), np.float32)), BF16)
    triu = jnp.asarray(np.triu(np.ones((TM, TM), np.float32)), BF16)
    place = jnp.asarray(_placement(), BF16)
    bs = jnp.repeat(b_s.T, GMLP_DG, axis=1).astype(F32)
    g_cq_s = row(g_cq * (dh_ca ** -0.5 * LOG2E))

    kcT, vc = _memkv(mem, row(g_mem), w_ckv, layer, row(g_ck))

    x1, (w2_in, w2_out, w_out_b, w_cq_b, w_co_b) = _ffn(
        x.reshape(t, d), row(g_ffn1), w_ffn1_in, w_ffn1_out, layer, (w_ffn2_in, w_ffn2_out, w_out, w_cq, w_co))
    qT, cT, k, cb, vT, yg = _proj(x1, batch, seq, row(g_mix), w_nn, w_nt, gq, gk, bfc, headsum, triu, place,
                                  row(g_sgu), w_s.astype(BF16), bs, row(g_gmlp_o))
    attn = _fox(qT, cT, k.reshape(batch, seq, FOX_W), cb.reshape(batch, seq, LANES), vT)
    out = _tail(x1, attn.reshape(t, FOX_W), yg, seq, row(g_fox_o), w_out_b, row(g_ca),
                w_cq_b, g_cq_s, kcT, vc, w_co_b, row(g_ffn2), w2_in, w2_out)
    return out.reshape(batch, seq, d)


def kernel(x, mem, g_ffn1, w_ffn1_in, w_ffn1_out, g_mix, w_in, b_f, g_q, g_k, g_sgu, w_s, b_s, g_fox_o,
           g_gmlp_o, w_out, g_ca, g_mem, w_cq, w_ckv, g_cq, g_ck, w_co, g_ffn2, w_ffn2_in, w_ffn2_out):
    layer_params = (g_ffn1, w_ffn1_in, w_ffn1_out, g_mix, w_in, b_f, g_q, g_k, g_sgu, w_s, b_s, g_fox_o,
                    g_gmlp_o, w_out, g_ca, g_mem, w_cq, w_ckv, g_cq, g_ck, w_co, g_ffn2, w_ffn2_in, w_ffn2_out)
    for layer in range(g_ffn1.shape[0]):
        x = _layer(layer, x, mem, *layer_params)
    return x
```

```python
import functools

import jax
import jax.numpy as jnp
import numpy as np
from jax import lax
from jax.experimental import pallas as pl
from jax.experimental.pallas import tpu as pltpu

F32 = jnp.float32
BF16 = jnp.bfloat16
EPS = 1e-6
LOG2E = 1.4426950408889634
MASKED = -1e30

FOX_HEADS = 8
FOX_DH = 64
FOX_W = FOX_HEADS * FOX_DH
GMLP_GROUPS = 8
GMLP_DG = 64
GMLP_W = GMLP_GROUPS * GMLP_DG
CHUNK = 128
CA_HEADS = 4

LANES = 128
VMEM_LIMIT_BYTES = 56 * 1024 * 1024

TM = 512
TQ = 512
TK = 512
FC = 256
PAIRS = FOX_HEADS // 2
BIAS_ROWS = 32
V_ROWS = FOX_DH + 16


def _inv_rms(x):
    return lax.rsqrt(jnp.mean(x * x, axis=-1, keepdims=True) + EPS)


def _rms(x):
    return x * _inv_rms(x)


def _dot(a, b):
    return lax.dot_general(a, b, (((1,), (0,)), ((), ())), preferred_element_type=F32)


def _dot_nt(a, b):
    return lax.dot_general(a, b, (((1,), (1,)), ((), ())), preferred_element_type=F32)


def _split3(x):
    hi = x.astype(BF16).astype(F32)
    mid = (x - hi).astype(BF16).astype(F32)
    lo = (x - hi - mid).astype(BF16).astype(F32)
    return hi, mid, lo


def _resident(shape):
    return pl.BlockSpec(shape, lambda *_: (0,) * len(shape), pipeline_mode=pl.Buffered(1))


def _params(n_axes):
    return pltpu.CompilerParams(dimension_semantics=("arbitrary",) * n_axes,
                                vmem_limit_bytes=VMEM_LIMIT_BYTES)


def _swiglu_half_step(x, g_ref, wgu_ref, wo_ref, act_ref):
    d_ff = wo_ref.shape[0]
    xg = (x * g_ref[...]).astype(BF16)
    inv = _inv_rms(x)
    for c in range(d_ff // FC):
        gate = _dot(xg, wgu_ref[:, c * FC:(c + 1) * FC]) * inv
        up = _dot(xg, wgu_ref[:, d_ff + c * FC:d_ff + (c + 1) * FC]) * inv
        act_ref[:, c * FC:(c + 1) * FC] = (gate * jax.nn.sigmoid(gate) * up).astype(BF16)
    return x + 0.5 * _dot(act_ref[...], wo_ref[...])


def _ffn_kernel(x_ref, g_ref, wgu_ref, wo_ref, *refs, n_cast):
    o_ref, act_ref = refs[n_cast], refs[-1]
    for src_ref, dst_ref in zip(refs[:n_cast], refs[n_cast + 1:-1]):
        dst_ref[...] = src_ref[0].astype(BF16)
    o_ref[...] = _swiglu_half_step(x_ref[...], g_ref, wgu_ref.at[0], wo_ref.at[0], act_ref)


def _cast_rows(rows, steps):
    return next(rb for rb in range(16, rows + 1, 16) if rows % rb == 0 and rows // rb <= steps)


def _ffn(x, g, wgu, wo, layer, to_bf16=()):
    t, d = x.shape
    d_ff = wo.shape[1]
    steps = t // TM
    row = pl.BlockSpec((TM, d), lambda r: (r, 0))

    def stacked(shape):
        return pl.BlockSpec((1,) + shape, lambda r: (layer, 0, 0), pipeline_mode=pl.Buffered(1))

    srcs, dsts = [], []
    for w in to_bf16:
        rb = _cast_rows(w.shape[1], steps)
        last = w.shape[1] // rb - 1
        srcs.append(pl.BlockSpec((1, rb, w.shape[2]), lambda r, last=last: (layer, jnp.minimum(r, last), 0)))
        dsts.append(pl.BlockSpec((rb, w.shape[2]), lambda r, last=last: (jnp.minimum(r, last), 0)))
    outs = pl.pallas_call(
        functools.partial(_ffn_kernel, n_cast=len(to_bf16)),
        grid=(steps,),
        in_specs=[row, _resident((1, d)), stacked((d, 2 * d_ff)), stacked((d_ff, d))] + srcs,
        out_specs=[row] + dsts,
        out_shape=[jax.ShapeDtypeStruct((t, d), F32)] + [jax.ShapeDtypeStruct(w.shape[1:], BF16) for w in to_bf16],
        scratch_shapes=[pltpu.VMEM((TM, d_ff), BF16)],
        compiler_params=_params(1),
        name="ffn",
    )(x, g, wgu, wo, *to_bf16)
    return outs[0], outs[1:]


def _gelu_tanh(x):
    c = float(np.sqrt(2.0 / np.pi))
    return x * (0.5 * jnp.tanh(x * (c + (c * 0.044715) * (x * x))) + 0.5)


def _proj_kernel(x_ref, g_mix_ref, w_nn_ref, w_nt_ref, gq_ref, gk_ref, bf_ref, headsum_ref, triu_ref,
                 place_ref, g_sgu_ref, ws_ref, bs_ref, g_go_ref,
                 qT_ref, cT_ref, k_ref, cb_ref, vT_ref, yg_ref,
                 carry_ref, sgu_ref, *, blocks_per_seq):
    @pl.when(pl.program_id(0) % blocks_per_seq == 0)
    def _():
        carry_ref[...] = jnp.zeros_like(carry_ref)

    h = (_rms(x_ref[...]) * g_mix_ref[...]).astype(BF16)

    z_vg = _dot(h, w_nn_ref[:, FOX_W + GMLP_W:])
    z_u = _dot(h, w_nn_ref[:, FOX_W:FOX_W + GMLP_W])
    zT = _dot_nt(w_nt_ref[...], h)
    k = _dot(h, w_nn_ref[:, :FOX_W])

    yg_ref[...] = _gmlp_branch(z_u, z_vg, g_sgu_ref, ws_ref, bs_ref, g_go_ref, sgu_ref)

    for hd in range(FOX_HEADS):
        blk = zT[hd * FOX_DH:(hd + 1) * FOX_DH, :]
        inv = lax.rsqrt(jnp.mean(blk * blk, axis=0, keepdims=True) + EPS)
        qT_ref[0, hd * FOX_DH:(hd + 1) * FOX_DH, :] = (blk * inv * gq_ref[...]).astype(BF16)
    ones_rows = jnp.where(lax.broadcasted_iota(jnp.int32, (V_ROWS - FOX_DH, TK), 0) == 0, 1.0, 0.0).astype(BF16)
    for c in range(TM // TK):
        for hd in range(FOX_HEADS):
            vT_ref[0, c, hd * V_ROWS:hd * V_ROWS + FOX_DH, :] = (
                zT[FOX_W + hd * FOX_DH:FOX_W + (hd + 1) * FOX_DH, c * TK:(c + 1) * TK].astype(BF16))
            vT_ref[0, c, hd * V_ROWS + FOX_DH:(hd + 1) * V_ROWS, :] = ones_rows

    f = zT[2 * FOX_W:2 * FOX_W + FOX_HEADS, :] + bf_ref[...]
    logf = (jnp.minimum(f, 0.0) - jnp.log1p(jnp.exp(-jnp.abs(f)))) * LOG2E
    zeros8 = jnp.zeros((8, TM), F32)
    parts = jnp.concatenate(_split3(logf) + (zeros8,), axis=0).astype(BF16)
    cs = _dot(parts, triu_ref[...])
    c = cs[0:8] + cs[8:16] + cs[16:24] + carry_ref[:, 0:1]
    carry_ref[...] = jnp.broadcast_to(c[:, TM - 1:TM], carry_ref.shape)
    cT_ref[0] = c

    one_row = jnp.where(lax.broadcasted_iota(jnp.int32, (8, TM), 0) == 0, 1.0, 0.0)
    cparts = jnp.concatenate(_split3(c) + (one_row,), axis=0).astype(BF16)
    cb_ref[...] = _dot(place_ref[...], cparts).T.astype(BF16)

    ksum = _dot((k * k).astype(BF16), headsum_ref[...])
    k_ref[...] = (k * lax.rsqrt(ksum * (1.0 / FOX_DH) + EPS) * gk_ref[...]).astype(BF16)


def _gmlp_branch(z_u, z_vg, g_sgu_ref, ws_ref, bs_ref, g_go_ref, sgu_ref):
    vg = _gelu_tanh(z_vg)
    vgn = (_rms(vg) * g_sgu_ref[...]).astype(BF16)
    u = _gelu_tanh(z_u)
    lane = lax.broadcasted_iota(jnp.int32, (CHUNK, LANES), 1)
    tril = (lax.broadcasted_iota(jnp.int32, (CHUNK, CHUNK), 0)
            >= lax.broadcasted_iota(jnp.int32, (CHUNK, CHUNK), 1))
    n_chunks = TM // CHUNK
    for p in range(GMLP_GROUPS // 2):
        lo_blocks, hi_blocks = [], []
        for ch in range(n_chunks):
            vp = vgn[ch * CHUNK:(ch + 1) * CHUNK, p * LANES:(p + 1) * LANES]
            lo_blocks.append(jnp.where(lane < GMLP_DG, vp, jnp.zeros_like(vp)))
            hi_blocks.append(jnp.where(lane >= GMLP_DG, vp, jnp.zeros_like(vp)))
        vcat = jnp.concatenate([jnp.concatenate(lo_blocks, axis=1),
                                jnp.concatenate(hi_blocks, axis=1)], axis=0)
        w0 = ws_ref[2 * p]
        w1 = ws_ref[2 * p + 1]
        wpair = jnp.concatenate([jnp.where(tril, w0, jnp.zeros_like(w0)),
                                 jnp.where(tril, w1, jnp.zeros_like(w1))], axis=1)
        mixed = _dot(wpair, vcat)
        bias = bs_ref[:, p * LANES:(p + 1) * LANES]
        for ch in range(n_chunks):
            sgu_ref[ch * CHUNK:(ch + 1) * CHUNK, p * LANES:(p + 1) * LANES] = (
                u[ch * CHUNK:(ch + 1) * CHUNK, p * LANES:(p + 1) * LANES]
                * (mixed[:, ch * LANES:(ch + 1) * LANES] + bias))
    return (_rms(sgu_ref[...]) * g_go_ref[...]).astype(BF16)


def _proj(x, batch, seq, g_mix, w_nn, w_nt, gq, gk, bfc, headsum, triu, place, g_sgu, ws, bs, g_go):
    t, d = x.shape
    bps = seq // TM
    n_nt = w_nt.shape[0]

    def rows(width):
        return pl.BlockSpec((TM, width), lambda r: (r, 0))

    kern = functools.partial(_proj_kernel, blocks_per_seq=bps)
    return pl.pallas_call(
        kern,
        grid=(t // TM,),
        in_specs=[rows(d), _resident((1, d)), _resident((d, w_nn.shape[1])), _resident((n_nt, d)),
                  _resident((FOX_DH, TM)), _resident((1, FOX_W)), _resident((FOX_HEADS, TM)),
                  _resident((FOX_W, FOX_W)), _resident((TM, TM)), _resident((PAIRS * BIAS_ROWS, 32)),
                  _resident((1, GMLP_W)), _resident((GMLP_GROUPS, CHUNK, CHUNK)),
                  _resident((CHUNK, GMLP_W)), _resident((1, GMLP_W))],
        out_specs=[pl.BlockSpec((1, FOX_W, TM), lambda r: (r // bps, 0, r % bps)),
                   pl.BlockSpec((1, FOX_HEADS, TM), lambda r: (r // bps, 0, r % bps)),
                   rows(FOX_W), rows(LANES),
                   pl.BlockSpec((1, TM // TK, FOX_HEADS * V_ROWS, TK), lambda r: (r // bps, r % bps, 0, 0)),
                   rows(GMLP_W)],
        out_shape=[jax.ShapeDtypeStruct((batch, FOX_W, seq), BF16),
                   jax.ShapeDtypeStruct((batch, FOX_HEADS, seq), F32),
                   jax.ShapeDtypeStruct((t, FOX_W), BF16),
                   jax.ShapeDtypeStruct((t, LANES), BF16),
                   jax.ShapeDtypeStruct((batch, seq // TK, FOX_HEADS * V_ROWS, TK), BF16),
                   jax.ShapeDtypeStruct((t, GMLP_W), BF16)],
        scratch_shapes=[pltpu.VMEM((FOX_HEADS, LANES), F32), pltpu.VMEM((TM, GMLP_W), F32)],
        compiler_params=_params(1),
        name="proj",
    )(x, g_mix, w_nn, w_nt, gq, gk, bfc, headsum, triu, place, g_sgu, ws, bs, g_go)


def _fox_kernel(qT_ref, cT_ref, k_ref, cb_ref, vT_ref, o_ref, qq_ref, s_ref, acc_ref):
    i = pl.program_id(1)

    row = lax.broadcasted_iota(jnp.int32, (BIAS_ROWS, TQ), 0)
    zeros_q = jnp.zeros((FOX_DH, TQ), BF16)
    for p in range(PAIRS):
        for e in range(2):
            hd = 2 * p + e
            qh = qT_ref[0, hd * FOX_DH:(hd + 1) * FOX_DH, :]
            hi, mid, lo = _split3(cT_ref[0, hd:hd + 1, :])
            blk = jnp.where((row >= 3 * e) & (row < 3 * e + 3), 1.0, 0.0)
            blk = jnp.where(row == 6, hi, blk)
            blk = jnp.where(row == 7, mid, blk)
            blk = jnp.where(row == 8, lo, blk)
            pieces = [qh, zeros_q] if e == 0 else [zeros_q, qh]
            if p > 0:
                pieces.append(jnp.zeros((BIAS_ROWS * p, TQ), BF16))
            pieces.append(blk.astype(BF16))
            if p + 1 < PAIRS:
                pieces.append(jnp.zeros((BIAS_ROWS * (PAIRS - 1 - p), TQ), BF16))
            qq_ref[p, :, e * TQ:(e + 1) * TQ] = jnp.concatenate(pieces, axis=0)

    def key_chunk(p, j):
        k0 = pl.multiple_of(j * TK, TK)
        return jnp.concatenate([k_ref[0, pl.ds(k0, TK), p * LANES:(p + 1) * LANES],
                                cb_ref[0, pl.ds(k0, TK), :]], axis=1)

    def scores(p, j):
        s = _dot(key_chunk(p, j), qq_ref[p])
        s_ref[p] = s
        return jnp.max(s, axis=0, keepdims=True)

    def absorb(p, j, smax, m_old):
        m = jnp.maximum(m_old, smax)
        alpha = jnp.exp2(m_old - m)
        pb = jnp.exp2(s_ref[p] - m).astype(BF16)
        for e in range(2):
            hd = 2 * p + e
            pv = _dot(vT_ref[0, j, hd * V_ROWS:(hd + 1) * V_ROWS, :], pb[:, e * TQ:(e + 1) * TQ])
            acc_ref[hd] = alpha[:, e * TQ:(e + 1) * TQ] * acc_ref[hd] + pv
        return m

    H = TQ // 2
    future = lax.broadcasted_iota(jnp.int32, (H, H), 0) > lax.broadcasted_iota(jnp.int32, (H, H), 1)

    def late(x):
        return jnp.concatenate([x[:, H:TQ], x[:, TQ + H:]], axis=1)

    def scores_diag(p):
        kk = key_chunk(p, i)
        qq = qq_ref[p]
        s_early = _dot(kk[:H], qq)
        s_late = _dot(kk[H:], late(qq))
        tops_max = []
        for e in range(2):
            sq = jnp.where(future, MASKED, s_early[:, e * TQ:e * TQ + H])
            s_ref[p, 0:H, e * TQ:e * TQ + H] = sq
            s_ref[p, 0:H, e * TQ + H:(e + 1) * TQ] = s_early[:, e * TQ + H:(e + 1) * TQ]
            lq = jnp.where(future, MASKED, s_late[:, e * H:(e + 1) * H])
            s_ref[p, H:2 * H, e * H:(e + 1) * H] = lq
            tops_max += [jnp.max(sq, axis=0, keepdims=True),
                         jnp.maximum(jnp.max(s_early[:, e * TQ + H:(e + 1) * TQ], axis=0, keepdims=True),
                                     jnp.max(lq, axis=0, keepdims=True))]
        return jnp.concatenate(tops_max, axis=1)

    def absorb_diag(p, m):
        p_early = jnp.exp2(s_ref[p, 0:H, :] - m).astype(BF16)
        p_late = jnp.exp2(s_ref[p, H:2 * H, 0:TQ] - late(m)).astype(BF16)
        for e in range(2):
            hd = 2 * p + e
            vt = vT_ref[0, i, hd * V_ROWS:(hd + 1) * V_ROWS, :]
            pv_early = _dot(vt[:, :H], p_early[:, e * TQ:(e + 1) * TQ])
            pv_late = _dot(vt[:, H:], p_late[:, e * H:(e + 1) * H])
            acc_ref[hd] = jnp.concatenate([pv_early[:, :H], pv_early[:, H:] + pv_late], axis=1)
        return m

    def chunk(j, carry):
        smax, ms = carry
        new_m = []
        for p in range(PAIRS):
            nxt_smax = scores(p + 1, j) if p + 1 < PAIRS else scores(0, j + 1)
            new_m.append(absorb(p, j, smax, ms[p]))
            smax = nxt_smax
        return smax, tuple(new_m)

    ms = []
    smax = scores_diag(0)
    for p in range(PAIRS):
        nxt_smax = scores_diag(p + 1) if p + 1 < PAIRS else scores(0, 0)
        ms.append(absorb_diag(p, smax))
        smax = nxt_smax
    carry = (smax, tuple(ms))
    lax.fori_loop(0, i, chunk, carry)

    outs = []
    for hd in range(FOX_HEADS):
        acc = acc_ref[hd]
        outs.append(acc[:FOX_DH] * (1.0 / acc[FOX_DH:FOX_DH + 1]))
    o_ref[0] = jnp.concatenate(outs, axis=0).astype(BF16).T


def _fox(qT, cT, k, cb, vT):
    batch, _, seq = qT.shape
    return pl.pallas_call(
        _fox_kernel,
        grid=(batch, seq // TQ),
        in_specs=[pl.BlockSpec((1, FOX_W, TQ), lambda b, i: (b, 0, i)),
                  pl.BlockSpec((1, FOX_HEADS, TQ), lambda b, i: (b, 0, i)),
                  pl.BlockSpec((1, seq, FOX_W), lambda b, i: (b, 0, 0)),
                  pl.BlockSpec((1, seq, LANES), lambda b, i: (b, 0, 0)),
                  pl.BlockSpec((1, seq // TK, FOX_HEADS * V_ROWS, TK), lambda b, i: (b, 0, 0, 0))],
        out_specs=pl.BlockSpec((1, TQ, FOX_W), lambda b, i: (b, i, 0)),
        out_shape=jax.ShapeDtypeStruct((batch, seq, FOX_W), BF16),
        scratch_shapes=[pltpu.VMEM((PAIRS, 2 * LANES, 2 * TQ), BF16),
                        pltpu.VMEM((PAIRS, TK, 2 * TQ), F32),
                        pltpu.VMEM((FOX_HEADS, V_ROWS, TQ), F32)],
        compiler_params=_params(2),
        name="fox",
    )(qT, cT, k, cb, vT)


def _memkv_kernel(mem_ref, g_mem_ref, w_ckv_ref, g_ck_ref, kcT_ref, vc_ref):
    mn = (_rms(mem_ref[0]) * g_mem_ref[...]).astype(BF16)
    d = mn.shape[1]
    dh = d // CA_HEADS
    k = _dot(mn, w_ckv_ref[0, :, :d])
    for hd in range(CA_HEADS):
        kh = _rms(k[:, hd * dh:(hd + 1) * dh]) * g_ck_ref[...]
        kcT_ref[0, hd * dh:(hd + 1) * dh, :] = kh.T.astype(BF16)
    vc_ref[0] = _dot(mn, w_ckv_ref[0, :, d:]).astype(BF16)


def _memkv(mem, g_mem, w_ckv, layer, g_ck):
    batch, m, d = mem.shape
    return pl.pallas_call(
        _memkv_kernel,
        grid=(batch,),
        in_specs=[pl.BlockSpec((1, m, d), lambda b: (b, 0, 0)), _resident((1, d)),
                  pl.BlockSpec((1, d, 2 * d), lambda b: (layer, 0, 0), pipeline_mode=pl.Buffered(1)),
                  _resident((1, d // CA_HEADS))],
        out_specs=[pl.BlockSpec((1, d, m), lambda b: (b, 0, 0)), pl.BlockSpec((1, m, d), lambda b: (b, 0, 0))],
        out_shape=[jax.ShapeDtypeStruct((batch, d, m), BF16), jax.ShapeDtypeStruct((batch, m, d), BF16)],
        compiler_params=_params(1),
        name="memkv",
    )(mem, g_mem, w_ckv, g_ck)


def _tail_kernel(x_ref, attn_ref, yg_ref, g_fo_ref, w_out_ref, g_ca_ref, w_cq_ref, g_cq_ref, kcT_ref, vc_ref,
                 w_co_ref, g_f2_ref, wgu_ref, wo_ref, o_ref, act_ref):
    x2 = x_ref[...] + _dot(yg_ref[...], w_out_ref[FOX_W:, :])
    a = (_rms(attn_ref[...].astype(F32)) * g_fo_ref[...]).astype(BF16)
    x2 = x2 + _dot(a, w_out_ref[:FOX_W, :])

    hq = (_rms(x2) * g_ca_ref[...]).astype(BF16)
    qc = _dot(hq, w_cq_ref[...])
    dh = qc.shape[1] // CA_HEADS
    heads = [slice(hd * dh, (hd + 1) * dh) for hd in range(CA_HEADS)]
    qh = [(_rms(qc[:, c]) * g_cq_ref[...]).astype(BF16) for c in heads]
    s = [_dot(q, kcT_ref[0, c, :]) for q, c in zip(qh, heads)]
    pexp = [jnp.exp2(v - jnp.max(v, axis=-1, keepdims=True)) for v in s]
    outs = [(_dot(p.astype(BF16), vc_ref[0, :, c]) * (1.0 / jnp.sum(p, axis=-1, keepdims=True))).astype(BF16)
            for p, c in zip(pexp, heads)]
    x3 = x2 + _dot(jnp.concatenate(outs, axis=1), w_co_ref[...])

    o_ref[...] = _swiglu_half_step(x3, g_f2_ref, wgu_ref, wo_ref, act_ref)


def _tail(x1, attn, yg, seq, g_fo, w_out, g_ca, w_cq, g_cq, kcT, vc, w_co, g_f2, wgu, wo):
    t, d = x1.shape
    d_ff = wo.shape[0]
    m = vc.shape[1]
    bps = seq // TM

    def rows(width):
        return pl.BlockSpec((TM, width), lambda r: (r, 0))

    return pl.pallas_call(
        _tail_kernel,
        grid=(t // TM,),
        in_specs=[rows(d), rows(FOX_W), rows(GMLP_W), _resident((1, FOX_W)), _resident((d, d)),
                  _resident((1, d)), _resident((d, d)), _resident((1, d // CA_HEADS)),
                  pl.BlockSpec((1, d, m), lambda r: (r // bps, 0, 0)),
                  pl.BlockSpec((1, m, d), lambda r: (r // bps, 0, 0)),
                  _resident((d, d)), _resident((1, d)), _resident((d, 2 * d_ff)), _resident((d_ff, d))],
        out_specs=rows(d),
        out_shape=jax.ShapeDtypeStruct((t, d), F32),
        scratch_shapes=[pltpu.VMEM((TM, d_ff), BF16)],
        compiler_params=_params(1),
        name="tail",
    )(x1, attn, yg, g_fo, w_out, g_ca, w_cq, g_cq, kcT, vc, w_co, g_f2, wgu, wo)


def _placement():
    pm = np.zeros((PAIRS * BIAS_ROWS, 32), np.float32)
    for p in range(PAIRS):
        for e in range(2):
            for part in range(3):
                pm[BIAS_ROWS * p + 3 * e + part, 8 * part + 2 * p + e] = -1.0
        pm[BIAS_ROWS * p + 6:BIAS_ROWS * p + 9, 24] = 1.0
    return pm


def _layer(layer, x, mem, g_ffn1, w_ffn1_in, w_ffn1_out, g_mix, w_in, b_f, g_q, g_k, g_sgu, w_s, b_s, g_fox_o,
           g_gmlp_o, w_out, g_ca, g_mem, w_cq, w_ckv, g_cq, g_ck, w_co, g_ffn2, w_ffn2_in, w_ffn2_out):
    (g_ffn1, g_mix, w_in, b_f, g_q, g_k, g_sgu, w_s, b_s, g_fox_o, g_gmlp_o, g_ca, g_mem, g_cq, g_ck,
     g_ffn2) = (v[layer] for v in (g_ffn1, g_mix, w_in, b_f, g_q, g_k, g_sgu, w_s, b_s, g_fox_o, g_gmlp_o, g_ca,
                                   g_mem, g_cq, g_ck, g_ffn2))
    batch, seq, d = x.shape
    t = batch * seq
    dh_ca = d // CA_HEADS

    def row(v):
        return v.reshape(1, -1).astype(F32)

    k_off, v_off, f_off, uv_off = FOX_W, 2 * FOX_W, 3 * FOX_W, 3 * FOX_W + FOX_HEADS
    w_nn = jnp.concatenate([w_in[:, k_off:v_off], w_in[:, uv_off:]], axis=1).astype(BF16)
    w_nt = jnp.concatenate([w_in[:, :k_off].T, w_in[:, v_off:f_off].T, w_in[:, f_off:uv_off].T,
                            jnp.zeros((8, d), F32)], axis=0).astype(BF16)
    gq = jnp.broadcast_to((g_q * (FOX_DH ** -0.5 * LOG2E))[:, None], (FOX_DH, TM)).astype(F32)
    gk = row(jnp.tile(g_k, FOX_HEADS))
    bfc = jnp.broadcast_to(b_f[:, None], (FOX_HEADS, TM)).astype(F32)
    headsum = jnp.asarray(np.kron(np.eye(FOX_HEADS, dtype=np.float32),
                                  np.ones((FOX_DH, FOX_DH), np.float32)), BF16)
    triu = jnp.asarray(np.triu(np.ones((TM, TM), np.float32)), BF16)
    place = jnp.asarray(_placement(), BF16)
    bs = jnp.repeat(b_s.T, GMLP_DG, axis=1).astype(F32)
    g_cq_s = row(g_cq * (dh_ca ** -0.5 * LOG2E))

    kcT, vc = _memkv(mem, row(g_mem), w_ckv, layer, row(g_ck))

    x1, (w2_in, w2_out, w_out_b, w_cq_b, w_co_b) = _ffn(
        x.reshape(t, d), row(g_ffn1), w_ffn1_in, w_ffn1_out, layer, (w_ffn2_in, w_ffn2_out, w_out, w_cq, w_co))
    qT, cT, k, cb, vT, yg = _proj(x1, batch, seq, row(g_mix), w_nn, w_nt, gq, gk, bfc, headsum, triu, place,
                                  row(g_sgu), w_s.astype(BF16), bs, row(g_gmlp_o))
    attn = _fox(qT, cT, k.reshape(batch, seq, FOX_W), cb.reshape(batch, seq, LANES), vT)
    out = _tail(x1, attn.reshape(t, FOX_W), yg, seq, row(g_fox_o), w_out_b, row(g_ca),
                w_cq_b, g_cq_s, kcT, vc, w_co_b, row(g_ffn2), w2_in, w2_out)
    return out.reshape(batch, seq, d)


def kernel(x, mem, g_ffn1, w_ffn1_in, w_ffn1_out, g_mix, w_in, b_f, g_q, g_k, g_sgu, w_s, b_s, g_fox_o,
           g_gmlp_o, w_out, g_ca, g_mem, w_cq, w_ckv, g_cq, g_ck, w_co, g_ffn2, w_ffn2_in, w_ffn2_out):
    layer_params = (g_ffn1, w_ffn1_in, w_ffn1_out, g_mix, w_in, b_f, g_q, g_k, g_sgu, w_s, b_s, g_fox_o,
                    g_gmlp_o, w_out, g_ca, g_mem, w_cq, w_ckv, g_cq, g_ck, w_co, g_ffn2, w_ffn2_in, w_ffn2_out)
    for layer in range(g_ffn1.shape[0]):
        x = _layer(layer, x, mem, *layer_params)
    return x
```

```python
import functools

import jax
import jax.numpy as jnp
import numpy as np
from jax import lax
from jax.experimental import pallas as pl
from jax.experimental.pallas import tpu as pltpu

F32 = jnp.float32
BF16 = jnp.bfloat16
EPS = 1e-6
LOG2E = 1.4426950408889634
MASKED = -1e30

FOX_HEADS = 8
FOX_DH = 64
FOX_W = FOX_HEADS * FOX_DH
GMLP_GROUPS = 8
GMLP_DG = 64
GMLP_W = GMLP_GROUPS * GMLP_DG
CHUNK = 128
CA_HEADS = 4

LANES = 128
VMEM_LIMIT_BYTES = 56 * 1024 * 1024

TM = 512
TQ = 512
TK = 512
FC = 256
PAIRS = FOX_HEADS // 2
BIAS_ROWS = 32
V_ROWS = FOX_DH + 16


def _inv_rms(x):
    return lax.rsqrt(jnp.mean(x * x, axis=-1, keepdims=True) + EPS)


def _rms(x):
    return x * _inv_rms(x)


def _dot(a, b):
    return lax.dot_general(a, b, (((1,), (0,)), ((), ())), preferred_element_type=F32)


def _dot_nt(a, b):
    return lax.dot_general(a, b, (((1,), (1,)), ((), ())), preferred_element_type=F32)


def _split3(x):
    hi = x.astype(BF16).astype(F32)
    mid = (x - hi).astype(BF16).astype(F32)
    lo = (x - hi - mid).astype(BF16).astype(F32)
    return hi, mid, lo


def _resident(shape):
    return pl.BlockSpec(shape, lambda *_: (0,) * len(shape), pipeline_mode=pl.Buffered(1))


def _params(n_axes):
    return pltpu.CompilerParams(dimension_semantics=("arbitrary",) * n_axes,
                                vmem_limit_bytes=VMEM_LIMIT_BYTES)


def _swiglu_half_step(x, g_ref, wgu_ref, wo_ref, act_ref):
    d_ff = wo_ref.shape[0]
    xg = (x * g_ref[...]).astype(BF16)
    inv = _inv_rms(x)
    for c in range(d_ff // FC):
        gate = _dot(xg, wgu_ref[:, c * FC:(c + 1) * FC]) * inv
        up = _dot(xg, wgu_ref[:, d_ff + c * FC:d_ff + (c + 1) * FC]) * inv
        act_ref[:, c * FC:(c + 1) * FC] = (gate * jax.nn.sigmoid(gate) * up).astype(BF16)
    return x + 0.5 * _dot(act_ref[...], wo_ref[...])


def _ffn_kernel(x_ref, g_ref, wgu_ref, wo_ref, *refs, n_cast):
    o_ref, act_ref = refs[n_cast], refs[-1]
    for src_ref, dst_ref in zip(refs[:n_cast], refs[n_cast + 1:-1]):
        dst_ref[...] = src_ref[0].astype(BF16)
    o_ref[...] = _swiglu_half_step(x_ref[...], g_ref, wgu_ref.at[0], wo_ref.at[0], act_ref)


def _cast_rows(rows, steps):
    return next(rb for rb in range(16, rows + 1, 16) if rows % rb == 0 and rows // rb <= steps)


def _ffn(x, g, wgu, wo, layer, to_bf16=()):
    t, d = x.shape
    d_ff = wo.shape[1]
    steps = t // TM
    row = pl.BlockSpec((TM, d), lambda r: (r, 0))

    def stacked(shape):
        return pl.BlockSpec((1,) + shape, lambda r: (layer, 0, 0), pipeline_mode=pl.Buffered(1))

    srcs, dsts = [], []
    for w in to_bf16:
        rb = _cast_rows(w.shape[1], steps)
        last = w.shape[1] // rb - 1
        srcs.append(pl.BlockSpec((1, rb, w.shape[2]), lambda r, last=last: (layer, jnp.minimum(r, last), 0)))
        dsts.append(pl.BlockSpec((rb, w.shape[2]), lambda r, last=last: (jnp.minimum(r, last), 0)))
    outs = pl.pallas_call(
        functools.partial(_ffn_kernel, n_cast=len(to_bf16)),
        grid=(steps,),
        in_specs=[row, _resident((1, d)), stacked((d, 2 * d_ff)), stacked((d_ff, d))] + srcs,
        out_specs=[row] + dsts,
        out_shape=[jax.ShapeDtypeStruct((t, d), F32)] + [jax.ShapeDtypeStruct(w.shape[1:], BF16) for w in to_bf16],
        scratch_shapes=[pltpu.VMEM((TM, d_ff), BF16)],
        compiler_params=_params(1),
        name="ffn",
    )(x, g, wgu, wo, *to_bf16)
    return outs[0], outs[1:]


def _gelu_tanh(x):
    c = float(np.sqrt(2.0 / np.pi))
    return x * (0.5 * jnp.tanh(x * (c + (c * 0.044715) * (x * x))) + 0.5)


def _proj_kernel(x_ref, g_mix_ref, w_nn_ref, w_nt_ref, gq_ref, gk_ref, bf_ref, headsum_ref, triu_ref,
                 place_ref, g_sgu_ref, ws_ref, bs_ref, g_go_ref,
                 qT_ref, cT_ref, k_ref, cb_ref, vT_ref, yg_ref,
                 carry_ref, sgu_ref, *, blocks_per_seq):
    @pl.when(pl.program_id(0) % blocks_per_seq == 0)
    def _():
        carry_ref[...] = jnp.zeros_like(carry_ref)

    h = (_rms(x_ref[...]) * g_mix_ref[...]).astype(BF16)

    z_vg = _dot(h, w_nn_ref[:, FOX_W + GMLP_W:])
    z_u = _dot(h, w_nn_ref[:, FOX_W:FOX_W + GMLP_W])
    zT = _dot_nt(w_nt_ref[...], h)
    k = _dot(h, w_nn_ref[:, :FOX_W])

    yg_ref[...] = _gmlp_branch(z_u, z_vg, g_sgu_ref, ws_ref, bs_ref, g_go_ref, sgu_ref)

    for hd in range(FOX_HEADS):
        blk = zT[hd * FOX_DH:(hd + 1) * FOX_DH, :]
        inv = lax.rsqrt(jnp.mean(blk * blk, axis=0, keepdims=True) + EPS)
        qT_ref[0, hd * FOX_DH:(hd + 1) * FOX_DH, :] = (blk * inv * gq_ref[...]).astype(BF16)
    ones_rows = jnp.where(lax.broadcasted_iota(jnp.int32, (V_ROWS - FOX_DH, TK), 0) == 0, 1.0, 0.0).astype(BF16)
    for c in range(TM // TK):
        for hd in range(FOX_HEADS):
            vT_ref[0, c, hd * V_ROWS:hd * V_ROWS + FOX_DH, :] = (
                zT[FOX_W + hd * FOX_DH:FOX_W + (hd + 1) * FOX_DH, c * TK:(c + 1) * TK].astype(BF16))
            vT_ref[0, c, hd * V_ROWS + FOX_DH:(hd + 1) * V_ROWS, :] = ones_rows

    f = zT[2 * FOX_W:2 * FOX_W + FOX_HEADS, :] + bf_ref[...]
    logf = (jnp.minimum(f, 0.0) - jnp.log1p(jnp.exp(-jnp.abs(f)))) * LOG2E
    zeros8 = jnp.zeros((8, TM), F32)
    parts = jnp.concatenate(_split3(logf) + (zeros8,), axis=0).astype(BF16)
    cs = _dot(parts, triu_ref[...])
    c = cs[0:8] + cs[8:16] + cs[16:24] + carry_ref[:, 0:1]
    carry_ref[...] = jnp.broadcast_to(c[:, TM - 1:TM], carry_ref.shape)
    cT_ref[0] = c

    one_row = jnp.where(lax.broadcasted_iota(jnp.int32, (8, TM), 0) == 0, 1.0, 0.0)
    cparts = jnp.concatenate(_split3(c) + (one_row,), axis=0).astype(BF16)
    cb_ref[...] = _dot(place_ref[...], cparts).T.astype(BF16)

    ksum = _dot((k * k).astype(BF16), headsum_ref[...])
    k_ref[...] = (k * lax.rsqrt(ksum * (1.0 / FOX_DH) + EPS) * gk_ref[...]).astype(BF16)


def _gmlp_branch(z_u, z_vg, g_sgu_ref, ws_ref, bs_ref, g_go_ref, sgu_ref):
    vg = _gelu_tanh(z_vg)
    vgn = (_rms(vg) * g_sgu_ref[...]).astype(BF16)
    u = _gelu_tanh(z_u)
    lane = lax.broadcasted_iota(jnp.int32, (CHUNK, LANES), 1)
    tril = (lax.broadcasted_iota(jnp.int32, (CHUNK, CHUNK), 0)
            >= lax.broadcasted_iota(jnp.int32, (CHUNK, CHUNK), 1))
    n_chunks = TM // CHUNK
    for p in range(GMLP_GROUPS // 2):
        lo_blocks, hi_blocks = [], []
        for ch in range(n_chunks):
            vp = vgn[ch * CHUNK:(ch + 1) * CHUNK, p * LANES:(p + 1) * LANES]
            lo_blocks.append(jnp.where(lane < GMLP_DG, vp, jnp.zeros_like(vp)))
            hi_blocks.append(jnp.where(lane >= GMLP_DG, vp, jnp.zeros_like(vp)))
        vcat = jnp.concatenate([jnp.concatenate(lo_blocks, axis=1),
                                jnp.concatenate(hi_blocks, axis=1)], axis=0)
        w0 = ws_ref[2 * p]
        w1 = ws_ref[2 * p + 1]
        wpair = jnp.concatenate([jnp.where(tril, w0, jnp.zeros_like(w0)),
                                 jnp.where(tril, w1, jnp.zeros_like(w1))], axis=1)
        mixed = _dot(wpair, vcat)
        bias = bs_ref[:, p * LANES:(p + 1) * LANES]
        for ch in range(n_chunks):
            sgu_ref[ch * CHUNK:(ch + 1) * CHUNK, p * LANES:(p + 1) * LANES] = (
                u[ch * CHUNK:(ch + 1) * CHUNK, p * LANES:(p + 1) * LANES]
                * (mixed[:, ch * LANES:(ch + 1) * LANES] + bias))
    return (_rms(sgu_ref[...]) * g_go_ref[...]).astype(BF16)


def _proj(x, batch, seq, g_mix, w_nn, w_nt, gq, gk, bfc, headsum, triu, place, g_sgu, ws, bs, g_go):
    t, d = x.shape
    bps = seq // TM
    n_nt = w_nt.shape[0]

    def rows(width):
        return pl.BlockSpec((TM, width), lambda r: (r, 0))

    kern = functools.partial(_proj_kernel, blocks_per_seq=bps)
    return pl.pallas_call(
        kern,
        grid=(t // TM,),
        in_specs=[rows(d), _resident((1, d)), _resident((d, w_nn.shape[1])), _resident((n_nt, d)),
                  _resident((FOX_DH, TM)), _resident((1, FOX_W)), _resident((FOX_HEADS, TM)),
                  _resident((FOX_W, FOX_W)), _resident((TM, TM)), _resident((PAIRS * BIAS_ROWS, 32)),
                  _resident((1, GMLP_W)), _resident((GMLP_GROUPS, CHUNK, CHUNK)),
                  _resident((CHUNK, GMLP_W)), _resident((1, GMLP_W))],
        out_specs=[pl.BlockSpec((1, FOX_W, TM), lambda r: (r // bps, 0, r % bps)),
                   pl.BlockSpec((1, FOX_HEADS, TM), lambda r: (r // bps, 0, r % bps)),
                   rows(FOX_W), rows(LANES),
                   pl.BlockSpec((1, TM // TK, FOX_HEADS * V_ROWS, TK), lambda r: (r // bps, r % bps, 0, 0)),
                   rows(GMLP_W)],
        out_shape=[jax.ShapeDtypeStruct((batch, FOX_W, seq), BF16),
                   jax.ShapeDtypeStruct((batch, FOX_HEADS, seq), F32),
                   jax.ShapeDtypeStruct((t, FOX_W), BF16),
                   jax.ShapeDtypeStruct((t, LANES), BF16),
                   jax.ShapeDtypeStruct((batch, seq // TK, FOX_HEADS * V_ROWS, TK), BF16),
                   jax.ShapeDtypeStruct((t, GMLP_W), BF16)],
        scratch_shapes=[pltpu.VMEM((FOX_HEADS, LANES), F32), pltpu.VMEM((TM, GMLP_W), F32)],
        compiler_params=_params(1),
        name="proj",
    )(x, g_mix, w_nn, w_nt, gq, gk, bfc, headsum, triu, place, g_sgu, ws, bs, g_go)


def _fox_kernel(qT_ref, cT_ref, k_ref, cb_ref, vT_ref, o_ref, qq_ref, s_ref, acc_ref):
    i = pl.program_id(1)

    row = lax.broadcasted_iota(jnp.int32, (BIAS_ROWS, TQ), 0)
    zeros_q = jnp.zeros((FOX_DH, TQ), BF16)
    for p in range(PAIRS):
        for e in range(2):
            hd = 2 * p + e
            qh = qT_ref[0, hd * FOX_DH:(hd + 1) * FOX_DH, :]
            hi, mid, lo = _split3(cT_ref[0, hd:hd + 1, :])
            blk = jnp.where((row >= 3 * e) & (row < 3 * e + 3), 1.0, 0.0)
            blk = jnp.where(row == 6, hi, blk)
            blk = jnp.where(row == 7, mid, blk)
            blk = jnp.where(row == 8, lo, blk)
            pieces = [qh, zeros_q] if e == 0 else [zeros_q, qh]
            if p > 0:
                pieces.append(jnp.zeros((BIAS_ROWS * p, TQ), BF16))
            pieces.append(blk.astype(BF16))
            if p + 1 < PAIRS:
                pieces.append(jnp.zeros((BIAS_ROWS * (PAIRS - 1 - p), TQ), BF16))
            qq_ref[p, :, e * TQ:(e + 1) * TQ] = jnp.concatenate(pieces, axis=0)

    def key_chunk(p, j):
        k0 = pl.multiple_of(j * TK, TK)
        return jnp.concatenate([k_ref[0, pl.ds(k0, TK), p * LANES:(p + 1) * LANES],
                                cb_ref[0, pl.ds(k0, TK), :]], axis=1)

    def scores(p, j):
        s = _dot(key_chunk(p, j), qq_ref[p])
        s_ref[p] = s
        return jnp.max(s, axis=0, keepdims=True)

    def absorb(p, j, smax, m_old):
        m = jnp.maximum(m_old, smax)
        alpha = jnp.exp2(m_old - m)
        pb = jnp.exp2(s_ref[p] - m).astype(BF16)
        for e in range(2):
            hd = 2 * p + e
            pv = _dot(vT_ref[0, j, hd * V_ROWS:(hd + 1) * V_ROWS, :], pb[:, e * TQ:(e + 1) * TQ])
            acc_ref[hd] = alpha[:, e * TQ:(e + 1) * TQ] * acc_ref[hd] + pv
        return m

    H = TQ // 2
    future = lax.broadcasted_iota(jnp.int32, (H, H), 0) > lax.broadcasted_iota(jnp.int32, (H, H), 1)

    def late(x):
        return jnp.concatenate([x[:, H:TQ], x[:, TQ + H:]], axis=1)

    def scores_diag(p):
        kk = key_chunk(p, i)
        qq = qq_ref[p]
        s_early = _dot(kk[:H], qq)
        s_late = _dot(kk[H:], late(qq))
        tops_max = []
        for e in range(2):
            sq = jnp.where(future, MASKED, s_early[:, e * TQ:e * TQ + H])
            s_ref[p, 0:H, e * TQ:e * TQ + H] = sq
            s_ref[p, 0:H, e * TQ + H:(e + 1) * TQ] = s_early[:, e * TQ + H:(e + 1) * TQ]
            lq = jnp.where(future, MASKED, s_late[:, e * H:(e + 1) * H])
            s_ref[p, H:2 * H, e * H:(e + 1) * H] = lq
            tops_max += [jnp.max(sq, axis=0, keepdims=True),
                         jnp.maximum(jnp.max(s_early[:, e * TQ + H:(e + 1) * TQ], axis=0, keepdims=True),
                                     jnp.max(lq, axis=0, keepdims=True))]
        return jnp.concatenate(tops_max, axis=1)

    def absorb_diag(p, m):
        p_early = jnp.exp2(s_ref[p, 0:H, :] - m).astype(BF16)
        p_late = jnp.exp2(s_ref[p, H:2 * H, 0:TQ] - late(m)).astype(BF16)
        for e in range(2):
            hd = 2 * p + e
            vt = vT_ref[0, i, hd * V_ROWS:(hd + 1) * V_ROWS, :]
            pv_early = _dot(vt[:, :H], p_early[:, e * TQ:(e + 1) * TQ])
            pv_late = _dot(vt[:, H:], p_late[:, e * H:(e + 1) * H])
            acc_ref[hd] = jnp.concatenate([pv_early[:, :H], pv_early[:, H:] + pv_late], axis=1)
        return m

    def chunk(j, carry):
        smax, ms = carry
        new_m = []
        for p in range(PAIRS):
            nxt_smax = scores(p + 1, j) if p + 1 < PAIRS else scores(0, j + 1)
            new_m.append(absorb(p, j, smax, ms[p]))
            smax = nxt_smax
        return smax, tuple(new_m)

    ms = []
    smax = scores_diag(0)
    for p in range(PAIRS):
        nxt_smax = scores_diag(p + 1) if p + 1 < PAIRS else scores(0, 0)
        ms.append(absorb_diag(p, smax))
        smax = nxt_smax
    carry = (smax, tuple(ms))
    carry = lax.fori_loop(0, i // 2, lambda t, cr: chunk(2 * t + 1, chunk(2 * t, cr)), carry)

    @pl.when(i % 2 == 1)
    def _():
        chunk(i - 1, carry)

    outs = []
    for hd in range(FOX_HEADS):
        acc = acc_ref[hd]
        outs.append(acc[:FOX_DH] * (1.0 / acc[FOX_DH:FOX_DH + 1]))
    o_ref[0] = jnp.concatenate(outs, axis=0).astype(BF16).T


def _fox(qT, cT, k, cb, vT):
    batch, _, seq = qT.shape
    return pl.pallas_call(
        _fox_kernel,
        grid=(batch, seq // TQ),
        in_specs=[pl.BlockSpec((1, FOX_W, TQ), lambda b, i: (b, 0, i)),
                  pl.BlockSpec((1, FOX_HEADS, TQ), lambda b, i: (b, 0, i)),
                  pl.BlockSpec((1, seq, FOX_W), lambda b, i: (b, 0, 0)),
                  pl.BlockSpec((1, seq, LANES), lambda b, i: (b, 0, 0)),
                  pl.BlockSpec((1, seq // TK, FOX_HEADS * V_ROWS, TK), lambda b, i: (b, 0, 0, 0))],
        out_specs=pl.BlockSpec((1, TQ, FOX_W), lambda b, i: (b, i, 0)),
        out_shape=jax.ShapeDtypeStruct((batch, seq, FOX_W), BF16),
        scratch_shapes=[pltpu.VMEM((PAIRS, 2 * LANES, 2 * TQ), BF16),
                        pltpu.VMEM((PAIRS, TK, 2 * TQ), F32),
                        pltpu.VMEM((FOX_HEADS, V_ROWS, TQ), F32)],
        compiler_params=_params(2),
        name="fox",
    )(qT, cT, k, cb, vT)


def _memkv_kernel(mem_ref, g_mem_ref, w_ckv_ref, g_ck_ref, kcT_ref, vc_ref):
    mn = (_rms(mem_ref[0]) * g_mem_ref[...]).astype(BF16)
    d = mn.shape[1]
    dh = d // CA_HEADS
    k = _dot(mn, w_ckv_ref[0, :, :d])
    for hd in range(CA_HEADS):
        kh = _rms(k[:, hd * dh:(hd + 1) * dh]) * g_ck_ref[...]
        kcT_ref[0, hd * dh:(hd + 1) * dh, :] = kh.T.astype(BF16)
    vc_ref[0] = _dot(mn, w_ckv_ref[0, :, d:]).astype(BF16)


def _memkv(mem, g_mem, w_ckv, layer, g_ck):
    batch, m, d = mem.shape
    return pl.pallas_call(
        _memkv_kernel,
        grid=(batch,),
        in_specs=[pl.BlockSpec((1, m, d), lambda b: (b, 0, 0)), _resident((1, d)),
                  pl.BlockSpec((1, d, 2 * d), lambda b: (layer, 0, 0), pipeline_mode=pl.Buffered(1)),
                  _resident((1, d // CA_HEADS))],
        out_specs=[pl.BlockSpec((1, d, m), lambda b: (b, 0, 0)), pl.BlockSpec((1, m, d), lambda b: (b, 0, 0))],
        out_shape=[jax.ShapeDtypeStruct((batch, d, m), BF16), jax.ShapeDtypeStruct((batch, m, d), BF16)],
        compiler_params=_params(1),
        name="memkv",
    )(mem, g_mem, w_ckv, g_ck)


def _tail_kernel(x_ref, attn_ref, yg_ref, g_fo_ref, w_out_ref, g_ca_ref, w_cq_ref, g_cq_ref, kcT_ref, vc_ref,
                 w_co_ref, g_f2_ref, wgu_ref, wo_ref, o_ref, act_ref):
    x2 = x_ref[...] + _dot(yg_ref[...], w_out_ref[FOX_W:, :])
    a = (_rms(attn_ref[...].astype(F32)) * g_fo_ref[...]).astype(BF16)
    x2 = x2 + _dot(a, w_out_ref[:FOX_W, :])

    hq = (_rms(x2) * g_ca_ref[...]).astype(BF16)
    qc = _dot(hq, w_cq_ref[...])
    dh = qc.shape[1] // CA_HEADS
    heads = [slice(hd * dh, (hd + 1) * dh) for hd in range(CA_HEADS)]
    qh = [(qc[:, c] * g_cq_ref[...]).astype(BF16) for c in heads]
    s = [_dot(q, kcT_ref[0, c, :]) * _inv_rms(qc[:, c]) for q, c in zip(qh, heads)]
    pexp = [jnp.exp2(v - jnp.max(v, axis=-1, keepdims=True)) for v in s]
    outs = [(_dot(p.astype(BF16), vc_ref[0, :, c]) * (1.0 / jnp.sum(p, axis=-1, keepdims=True))).astype(BF16)
            for p, c in zip(pexp, heads)]
    x3 = x2 + _dot(jnp.concatenate(outs, axis=1), w_co_ref[...])

    o_ref[...] = _swiglu_half_step(x3, g_f2_ref, wgu_ref, wo_ref, act_ref)


def _tail(x1, attn, yg, seq, g_fo, w_out, g_ca, w_cq, g_cq, kcT, vc, w_co, g_f2, wgu, wo):
    t, d = x1.shape
    d_ff = wo.shape[0]
    m = vc.shape[1]
    bps = seq // TM

    def rows(width):
        return pl.BlockSpec((TM, width), lambda r: (r, 0))

    return pl.pallas_call(
        _tail_kernel,
        grid=(t // TM,),
        in_specs=[rows(d), rows(FOX_W), rows(GMLP_W), _resident((1, FOX_W)), _resident((d, d)),
                  _resident((1, d)), _resident((d, d)), _resident((1, d // CA_HEADS)),
                  pl.BlockSpec((1, d, m), lambda r: (r // bps, 0, 0)),
                  pl.BlockSpec((1, m, d), lambda r: (r // bps, 0, 0)),
                  _resident((d, d)), _resident((1, d)), _resident((d, 2 * d_ff)), _resident((d_ff, d))],
        out_specs=rows(d),
        out_shape=jax.ShapeDtypeStruct((t, d), F32),
        scratch_shapes=[pltpu.VMEM((TM, d_ff), BF16)],
        compiler_params=_params(1),
        name="tail",
    )(x1, attn, yg, g_fo, w_out, g_ca, w_cq, g_cq, kcT, vc, w_co, g_f2, wgu, wo)


def _placement():
    pm = np.zeros((PAIRS * BIAS_ROWS, 32), np.float32)
    for p in range(PAIRS):
        for e in range(2):
            for part in range(3):
                pm[BIAS_ROWS * p + 3 * e + part, 8 * part + 2 * p + e] = -1.0
        pm[BIAS_ROWS * p + 6:BIAS_ROWS * p + 9, 24] = 1.0
    return pm


def _layer(layer, x, mem, g_ffn1, w_ffn1_in, w_ffn1_out, g_mix, w_in, b_f, g_q, g_k, g_sgu, w_s, b_s, g_fox_o,
           g_gmlp_o, w_out, g_ca, g_mem, w_cq, w_ckv, g_cq, g_ck, w_co, g_ffn2, w_ffn2_in, w_ffn2_out):
    (g_ffn1, g_mix, w_in, b_f, g_q, g_k, g_sgu, w_s, b_s, g_fox_o, g_gmlp_o, g_ca, g_mem, g_cq, g_ck,
     g_ffn2) = (v[layer] for v in (g_ffn1, g_mix, w_in, b_f, g_q, g_k, g_sgu, w_s, b_s, g_fox_o, g_gmlp_o, g_ca,
                                   g_mem, g_cq, g_ck, g_ffn2))
    batch, seq, d = x.shape
    t = batch * seq
    dh_ca = d // CA_HEADS

    def row(v):
        return v.reshape(1, -1).astype(F32)

    k_off, v_off, f_off, uv_off = FOX_W, 2 * FOX_W, 3 * FOX_W, 3 * FOX_W + FOX_HEADS
    w_nn = jnp.concatenate([w_in[:, k_off:v_off], w_in[:, uv_off:]], axis=1).astype(BF16)
    w_nt = jnp.concatenate([w_in[:, :k_off].T, w_in[:, v_off:f_off].T, w_in[:, f_off:uv_off].T,
                            jnp.zeros((8, d), F32)], axis=0).astype(BF16)
    gq = jnp.broadcast_to((g_q * (FOX_DH ** -0.5 * LOG2E))[:, None], (FOX_DH, TM)).astype(F32)
    gk = row(jnp.tile(g_k, FOX_HEADS))
    bfc = jnp.broadcast_to(b_f[:, None], (FOX_HEADS, TM)).astype(F32)
    headsum = jnp.asarray(np.kron(np.eye(FOX_HEADS, dtype=np.float32),
                                  np.ones((FOX_DH, FOX_DH), np.float32)), BF16)
    triu = jnp.asarray(np.triu(np.ones((TM, TM), np.float32)), BF16)
    place = jnp.asarray(_placement(), BF16)
    bs = jnp.repeat(b_s.T, GMLP_DG, axis=1).astype(F32)
    g_cq_s = row(g_cq * (dh_ca ** -0.5 * LOG2E))

    kcT, vc = _memkv(mem, row(g_mem), w_ckv, layer, row(g_ck))

    x1, (w2_in, w2_out, w_out_b, w_cq_b, w_co_b) = _ffn(
        x.reshape(t, d), row(g_ffn1), w_ffn1_in, w_ffn1_out, layer, (w_ffn2_in, w_ffn2_out, w_out, w_cq, w_co))
    qT, cT, k, cb, vT, yg = _proj(x1, batch, seq, row(g_mix), w_nn, w_nt, gq, gk, bfc, headsum, triu, place,
                                  row(g_sgu), w_s.astype(BF16), bs, row(g_gmlp_o))
    attn = _fox(qT, cT, k.reshape(batch, seq, FOX_W), cb.reshape(batch, seq, LANES), vT)
    out = _tail(x1, attn.reshape(t, FOX_W), yg, seq, row(g_fox_o), w_out_b, row(g_ca),
                w_cq_b, g_cq_s, kcT, vc, w_co_b, row(g_ffn2), w2_in, w2_out)
    return out.reshape(batch, seq, d)


def kernel(x, mem, g_ffn1, w_ffn1_in, w_ffn1_out, g_mix, w_in, b_f, g_q, g_k, g_sgu, w_s, b_s, g_fox_o,
           g_gmlp_o, w_out, g_ca, g_mem, w_cq, w_ckv, g_cq, g_ck, w_co, g_ffn2, w_ffn2_in, w_ffn2_out):
    layer_params = (g_ffn1, w_ffn1_in, w_ffn1_out, g_mix, w_in, b_f, g_q, g_k, g_sgu, w_s, b_s, g_fox_o,
                    g_gmlp_o, w_out, g_ca, g_mem, w_cq, w_ckv, g_cq, g_ck, w_co, g_ffn2, w_ffn2_in, w_ffn2_out)
    for layer in range(g_ffn1.shape[0]):
        x = _layer(layer, x, mem, *layer_params)
    return x
```
